```python
import jax, jax.numpy as jnp
from jax import lax
import numpy as np

D_MODEL = 1024
BATCH = 8
SEQ = 4096
DEPTH = 1

GMLP_WIDTH = D_MODEL
GMLP_GROUPS = 4
GMLP_CHUNK = 128
N_HEADS = 8
HEAD_DIM = D_MODEL // N_HEADS
ATTN_WIDTH = N_HEADS * HEAD_DIM
MOBA_BLOCK = 256
MOBA_TOPK = 3
MOBA_QCHUNK = 16
N_EXPERTS = 32
TOP_K = 4
D_EXPERT = D_MODEL
SWIGLU_LIMIT = 7.0
SWIGLU_ALPHA = 1.702
MOE_BLOCK = 256
LN_EPS = 1e-5
DEEPNORM_ALPHA = (2 * DEPTH) ** 0.25
DEEPNORM_BETA = (8 * DEPTH) ** -0.25
IN_SPLITS = (GMLP_WIDTH, GMLP_WIDTH, ATTN_WIDTH, ATTN_WIDTH, ATTN_WIDTH, D_MODEL, D_MODEL)
IN_COLS = sum(IN_SPLITS)

kernel_name = "hybrid_gmlp_moba_moe_deepnorm"


def layer_norm(x, g, b):
    xf = x.astype(jnp.float32)
    mu = jnp.mean(xf, axis=-1, keepdims=True)
    var = jnp.mean(jnp.square(xf - mu), axis=-1, keepdims=True)
    y = (xf - mu) * lax.rsqrt(var + LN_EPS)
    return (y * g.astype(jnp.float32) + b.astype(jnp.float32)).astype(x.dtype)


def chunked_spatial_gating(u, v, w_s, b_s, ln_g, ln_b):
    B, S, _ = u.shape
    C = GMLP_CHUNK
    nc = S // C
    gd = GMLP_WIDTH // GMLP_GROUPS
    v = layer_norm(v, ln_g, ln_b).reshape(B, nc, C, GMLP_GROUPS, gd)
    causal = jnp.tril(jnp.ones((C, C), dtype=bool))
    w = jnp.where(causal[None], w_s, jnp.zeros_like(w_s))
    vs = jnp.einsum('gts,bcsgd->bctgd', w, v) + b_s.T[None, None, :, :, None]
    return u * vs.reshape(B, S, GMLP_WIDTH)


def moba_attention(q, k, v):
    B, H, S, dh = q.shape
    L = MOBA_BLOCK
    nb = -(-S // L)
    Sp = nb * L
    padw = ((0, 0), (0, 0), (0, Sp - S), (0, 0))
    q = jnp.pad(q, padw)
    k = jnp.pad(k, padw)
    v = jnp.pad(v, padw)
    kb = k.reshape(B, H, nb, L, dh)
    vb = v.reshape(B, H, nb, L, dh)
    k_mean = jnp.mean(kb.astype(jnp.float32), axis=3)
    gate = jnp.einsum('bhsd,bhnd->bhsn', q.astype(jnp.float32), k_mean)
    q_blk = jnp.arange(Sp) // L
    past = jnp.arange(nb)[None, :] < q_blk[:, None]
    gate = jnp.where(past, gate, -jnp.inf)
    topk = min(MOBA_TOPK, nb)
    _, sel = lax.top_k(gate, topk)
    sel_ok = sel < q_blk[:, None]
    scale = dh ** -0.5
    QC = MOBA_QCHUNK
    bi = jnp.arange(B)[:, None, None, None]
    hi = jnp.arange(H)[None, :, None, None]

    def query_chunk(c):
        s0 = c * QC
        qc = lax.dynamic_slice_in_dim(q, s0, QC, axis=2)
        idx = lax.dynamic_slice_in_dim(sel, s0, QC, axis=2)
        ok = lax.dynamic_slice_in_dim(sel_ok, s0, QC, axis=2)
        k_sel = kb[bi, hi, idx]
        v_sel = vb[bi, hi, idx]
        own = s0 // L
        k_own = lax.dynamic_index_in_dim(kb, own, axis=2, keepdims=False)
        v_own = lax.dynamic_index_in_dim(vb, own, axis=2, keepdims=False)
        s_sel = jnp.einsum('bhqd,bhqnld->bhqnl', qc, k_sel).astype(jnp.float32) * scale
        s_sel = jnp.where(ok[..., None], s_sel, -jnp.inf)
        s_own = jnp.einsum('bhqd,bhld->bhql', qc, k_own).astype(jnp.float32) * scale
        causal = (own * L + jnp.arange(L))[None, :] <= (s0 + jnp.arange(QC))[:, None]
        s_own = jnp.where(causal, s_own, -jnp.inf)
        s_all = jnp.concatenate([s_sel.reshape(B, H, QC, topk * L), s_own], axis=-1)
        p = jax.nn.softmax(s_all, axis=-1).astype(v.dtype)
        p_sel = p[..., :topk * L].reshape(B, H, QC, topk, L)
        p_own = p[..., topk * L:]
        return (jnp.einsum('bhqnl,bhqnld->bhqd', p_sel, v_sel)
                + jnp.einsum('bhql,bhld->bhqd', p_own, v_own))

    outs = lax.map(query_chunk, jnp.arange(Sp // QC))
    out = outs.transpose(1, 2, 0, 3, 4).reshape(B, H, Sp, dh)
    return out[:, :, :S]


def moe_ffn(h, w_router, b_router, w_up, b_up, w_down, b_down):
    B, S, D = h.shape
    x = h.reshape(-1, D)
    T = x.shape[0]
    F = D_EXPERT
    M = MOE_BLOCK
    logits = (x @ w_router).astype(jnp.float32) + b_router.astype(jnp.float32)
    top_val, top_idx = lax.top_k(logits, TOP_K)
    gate = jax.nn.softmax(top_val, axis=-1)
    A = T * TOP_K
    R = (-(-A // M)) * M + N_EXPERTS * M
    flat_e = top_idx.reshape(-1)
    flat_tok = jnp.arange(A, dtype=jnp.int32) // TOP_K
    flat_w = gate.reshape(-1)
    order = jnp.argsort(flat_e)
    e_sorted = flat_e[order]
    counts = jnp.bincount(flat_e, length=N_EXPERTS)
    padded = (counts + M - 1) // M * M
    start = jnp.cumsum(counts) - counts
    pend = jnp.cumsum(padded)
    pstart = pend - padded
    dest = pstart[e_sorted] + jnp.arange(A) - start[e_sorted]
    row_tok = jnp.zeros((R,), jnp.int32).at[dest].set(flat_tok[order])
    row_w = jnp.zeros((R,), jnp.float32).at[dest].set(flat_w[order])
    n_blocks = R // M
    block_e = jnp.minimum(jnp.searchsorted(pend, jnp.arange(n_blocks) * M, side='right'),
                          N_EXPERTS - 1)

    def expert_block(args):
        e, tok = args
        xe = x[tok]
        gu = xe @ w_up[e] + b_up[e]
        g = jnp.minimum(gu[:, :F], SWIGLU_LIMIT)
        u = jnp.clip(gu[:, F:], -SWIGLU_LIMIT, SWIGLU_LIMIT)
        act = (u + 1.0) * (g * jax.nn.sigmoid(SWIGLU_ALPHA * g))
        return act @ w_down[e] + b_down[e]

    y = lax.map(expert_block, (block_e, row_tok.reshape(n_blocks, M)))
    y = y.reshape(R, D) * row_w[:, None].astype(y.dtype)
    out = jnp.zeros_like(x).at[row_tok].add(y)
    return out.reshape(B, S, D)


def setup_inputs(seed: int = 0) -> dict:
    key = jax.random.key(seed)
    ks = jax.random.split(key, 24)
    f32 = jnp.float32

    def nrm(k, shape, scale):
        return jax.random.normal(k, shape, f32) * scale

    D, L, E, F = D_MODEL, DEPTH, N_EXPERTS, D_EXPERT
    C, G = GMLP_CHUNK, GMLP_GROUPS
    return {
        "x": nrm(ks[0], (BATCH, SEQ, D), 1.0),
        "ln_in_g": 1.0 + nrm(ks[1], (D,), 0.02),
        "ln_in_b": nrm(ks[2], (D,), 0.02),
        "w_in": nrm(ks[3], (L, D, IN_COLS), D ** -0.5),
        "gmlp_ln_g": 1.0 + nrm(ks[4], (L, GMLP_WIDTH), 0.02),
        "gmlp_ln_b": nrm(ks[5], (L, GMLP_WIDTH), 0.02),
        "w_spatial": nrm(ks[6], (L, G, C, C), C ** -0.5),
        "b_spatial": 1.0 + nrm(ks[7], (L, G, C), 0.02),
        "w_branch_a": nrm(ks[8], (L, GMLP_WIDTH, D), GMLP_WIDTH ** -0.5),
        "w_branch_b": nrm(ks[9], (L, ATTN_WIDTH, D), ATTN_WIDTH ** -0.5),
        "w_out": nrm(ks[10], (L, D, D), DEEPNORM_BETA * D ** -0.5),
        "ln_mix_g": 1.0 + nrm(ks[11], (L, D), 0.02),
        "ln_mix_b": nrm(ks[12], (L, D), 0.02),
        "w_router": nrm(ks[13], (L, D, E), D ** -0.5),
        "b_router": nrm(ks[14], (L, E), 0.01),
        "w_up": nrm(ks[15], (L, E, D, 2 * F), D ** -0.5),
        "b_up": nrm(ks[16], (L, E, 2 * F), 0.01),
        "w_down": nrm(ks[17], (L, E, F, D), DEEPNORM_BETA * F ** -0.5),
        "b_down": nrm(ks[18], (L, E, D), 0.01),
        "ln_ffn_g": 1.0 + nrm(ks[19], (L, D), 0.02),
        "ln_ffn_b": nrm(ks[20], (L, D), 0.02),
    }


def reference(x, ln_in_g, ln_in_b, w_in, gmlp_ln_g, gmlp_ln_b, w_spatial, b_spatial,
              w_branch_a, w_branch_b, w_out, ln_mix_g, ln_mix_b, w_router, b_router,
              w_up, b_up, w_down, b_down, ln_ffn_g, ln_ffn_b):
    B, S, D = x.shape
    offsets = list(np.cumsum(IN_SPLITS)[:-1])
    h = layer_norm(x, ln_in_g, ln_in_b)
    for l in range(DEPTH):
        p = h @ w_in[l]
        u_a, v_a, q, k, v, g_a, g_b = jnp.split(p, offsets, axis=-1)
        y_a = chunked_spatial_gating(jax.nn.gelu(u_a, approximate=False),
                                     jax.nn.gelu(v_a, approximate=False),
                                     w_spatial[l], b_spatial[l], gmlp_ln_g[l], gmlp_ln_b[l])
        def heads(t):
            return t.reshape(B, S, N_HEADS, HEAD_DIM).transpose(0, 2, 1, 3)
        y_b = moba_attention(heads(q), heads(k), heads(v))
        y_b = y_b.transpose(0, 2, 1, 3).reshape(B, S, ATTN_WIDTH)
        merged = (jax.nn.sigmoid(g_a) * (y_a @ w_branch_a[l])
                  + jax.nn.sigmoid(g_b) * (y_b @ w_branch_b[l]))
        h = layer_norm(DEEPNORM_ALPHA * h + merged @ w_out[l], ln_mix_g[l], ln_mix_b[l])
        f = moe_ffn(h, w_router[l], b_router[l], w_up[l], b_up[l], w_down[l], b_down[l])
        h = layer_norm(DEEPNORM_ALPHA * h + f, ln_ffn_g[l], ln_ffn_b[l])
    return h
```

```python
import functools
import math

import jax
import jax.numpy as jnp
from jax import lax
from jax.experimental import pallas as pl
from jax.experimental.pallas import tpu as pltpu

F32 = jnp.float32
BF16 = jnp.bfloat16
I32 = jnp.int32

LN_EPS = 1e-5
N_HEADS = 8
MOBA_BLOCK = 256
MOBA_TOPK = 3
TOP_K = 4
SWIGLU_LIMIT = 7.0
SWIGLU_ALPHA = 1.702
LANES = 128
SUBLANES = 8
MOE_ROWS = 256
ROUTE_ROWS = 256
VMEM_LIMIT = 48 * 1024 * 1024


def _ln_rows(x, g, b):
    mu = jnp.mean(x, axis=-1, keepdims=True)
    xc = x - mu
    var = jnp.mean(xc * xc, axis=-1, keepdims=True)
    return xc * lax.rsqrt(var + LN_EPS) * g + b


def _gelu(x):
    return x * (lax.erf(x / math.sqrt(2.0)) + 1.0) / 2.0


def _sigmoid(x):
    return 1.0 / (1.0 + jnp.exp(-x))


def _params(*sem):
    return pltpu.CompilerParams(dimension_semantics=sem, vmem_limit_bytes=VMEM_LIMIT)


def _inproj_kernel(x_ref, g_ref, b_ref, w_ref, o_ref, h_scr):
    @pl.when(pl.program_id(1) == 0)
    def _():
        h_scr[...] = _ln_rows(x_ref[...], g_ref[...], b_ref[...]).astype(BF16)

    o_ref[...] = jnp.dot(h_scr[...], w_ref[...], preferred_element_type=F32).astype(BF16)


def _in_proj(x2, ln_g, ln_b, w_in, tm, tn):
    T, D = x2.shape
    N = w_in.shape[1]
    return pl.pallas_call(
        _inproj_kernel,
        grid=(T // tm, N // tn),
        in_specs=[
            pl.BlockSpec((tm, D), lambda i, j: (i, 0)),
            pl.BlockSpec((1, D), lambda i, j: (0, 0)),
            pl.BlockSpec((1, D), lambda i, j: (0, 0)),
            pl.BlockSpec((D, tn), lambda i, j: (0, j)),
        ],
        out_specs=pl.BlockSpec((tm, tn), lambda i, j: (i, j)),
        out_shape=jax.ShapeDtypeStruct((T, N), BF16),
        scratch_shapes=[pltpu.VMEM((tm, D), BF16)],
        compiler_params=_params("parallel", "arbitrary"),
        name="in_proj",
    )(x2, ln_g, ln_b, w_in)


def _mixer_a_kernel(u_ref, v_ref, lng_ref, lnb_ref, ws_ref, bs_ref, o_ref, *, chunk, groups):
    tr, width = u_ref.shape
    gd = width // groups
    u = _gelu(u_ref[...].astype(F32))
    v = _gelu(v_ref[...].astype(F32))
    vn = _ln_rows(v, lng_ref[...], lnb_ref[...]).astype(BF16)
    row = lax.broadcasted_iota(I32, (chunk, chunk), 0)
    col = lax.broadcasted_iota(I32, (chunk, chunk), 1)
    causal = col <= row
    for g in range(groups):
        w = jnp.where(causal, ws_ref[g], 0.0).astype(BF16)
        for c in range(tr // chunk):
            rs = slice(c * chunk, (c + 1) * chunk)
            cs = slice(g * gd, (g + 1) * gd)
            vs = jnp.dot(w, vn[rs, cs], preferred_element_type=F32) + bs_ref[g]
            o_ref[rs, cs] = (u[rs, cs] * vs).astype(BF16)


def _mixer_a(p, ln_g, ln_b, w_s, b_s, width, tr):
    T = p.shape[0]
    groups, chunk, _ = w_s.shape
    gd = width // groups
    bias = jnp.broadcast_to(b_s[:, :, None], (groups, chunk, gd))
    return pl.pallas_call(
        functools.partial(_mixer_a_kernel, chunk=chunk, groups=groups),
        grid=(T // tr,),
        in_specs=[
            pl.BlockSpec((tr, width), lambda i: (i, 0)),
            pl.BlockSpec((tr, width), lambda i: (i, 1)),
            pl.BlockSpec((1, width), lambda i: (0, 0)),
            pl.BlockSpec((1, width), lambda i: (0, 0)),
            pl.BlockSpec((groups, chunk, chunk), lambda i: (0, 0, 0)),
            pl.BlockSpec((groups, chunk, gd), lambda i: (0, 0, 0)),
        ],
        out_specs=pl.BlockSpec((tr, width), lambda i: (i, 0)),
        out_shape=jax.ShapeDtypeStruct((T, width), BF16),
        compiler_params=_params("parallel"),
        name="mixer_a",
    )(p, p, ln_g, ln_b, w_s, bias)


_NT = (((1,), (1,)), ((), ()))


def _moba_kernel(q_ref, k_ref, v_ref, o_ref, kmean_scr, vt_scr, bias_scr, *, blk, topk, scale):
    S, dh = q_ref.shape
    nb = S // blk
    for n in range(nb):
        rs = slice(n * blk, (n + 1) * blk)
        kmean_scr[n:n + 1, :] = jnp.mean(k_ref[rs, :].astype(F32), axis=0, keepdims=True)
        vt_scr[n] = v_ref[rs, :].astype(F32).T.astype(BF16)

    nidx = lax.broadcasted_iota(I32, (nb, blk), 0)
    kidx = lax.broadcasted_iota(I32, (blk, blk), 0)
    qidx = lax.broadcasted_iota(I32, (blk, blk), 1)

    def q_block(qb, carry):
        q0 = pl.multiple_of(qb * blk, blk)
        q = q_ref[pl.ds(q0, blk), :]
        gate = lax.dot_general(kmean_scr[...].astype(BF16), q, _NT, preferred_element_type=F32)
        past = nidx < qb
        gate = jnp.where(past, gate, -jnp.inf)
        rank = jnp.zeros((nb, blk), I32)
        for m in range(nb):
            gm = gate[m:m + 1, :]
            ahead = jnp.where(gm > gate, 1, jnp.where((gm == gate) & (nidx > m), 1, 0))
            rank = rank + ahead
        chosen = past & (rank < topk)
        bias_scr[...] = jnp.where(chosen, 0.0, -jnp.inf)

        k_own = k_ref[pl.ds(q0, blk), :]
        s = lax.dot_general(k_own, q, _NT, preferred_element_type=F32) * scale
        s = jnp.where(kidx <= qidx, s, -jnp.inf)
        m0 = jnp.max(s, axis=0, keepdims=True)
        p = jnp.exp(s - m0)
        l0 = jnp.sum(p, axis=0, keepdims=True)
        acc0 = jnp.dot(vt_scr[qb], p.astype(BF16), preferred_element_type=F32)

        def past_block(n, c):
            m_run, l_run, acc = c
            n0 = pl.multiple_of(n * blk, blk)
            kb = k_ref[pl.ds(n0, blk), :]
            s = lax.dot_general(kb, q, _NT, preferred_element_type=F32) * scale
            s = s + bias_scr[pl.ds(n, 1), :]
            m_new = jnp.maximum(m_run, jnp.max(s, axis=0, keepdims=True))
            alpha = jnp.exp(m_run - m_new)
            p = jnp.exp(s - m_new)
            l_new = alpha * l_run + jnp.sum(p, axis=0, keepdims=True)
            acc_new = alpha * acc + jnp.dot(vt_scr[n], p.astype(BF16), preferred_element_type=F32)
            return m_new, l_new, acc_new

        _, l_fin, acc = lax.fori_loop(0, qb, past_block, (m0, l0, acc0))
        o_ref[pl.ds(q0, blk), :] = (acc / l_fin).T.astype(BF16)
        return carry

    lax.fori_loop(0, nb, q_block, 0)


def _moba(p, B, S, n_heads, dh, q_col, k_col, v_col):
    T = p.shape[0]
    nb = S // MOBA_BLOCK
    kern = functools.partial(_moba_kernel, blk=MOBA_BLOCK, topk=MOBA_TOPK, scale=dh ** -0.5)
    return pl.pallas_call(
        kern,
        grid=(B, n_heads),
        in_specs=[
            pl.BlockSpec((S, dh), lambda b, h: (b, q_col + h)),
            pl.BlockSpec((S, dh), lambda b, h: (b, k_col + h)),
            pl.BlockSpec((S, dh), lambda b, h: (b, v_col + h)),
        ],
        out_specs=pl.BlockSpec((S, dh), lambda b, h: (b, h)),
        out_shape=jax.ShapeDtypeStruct((T, n_heads * dh), BF16),
        scratch_shapes=[
            pltpu.VMEM((nb, dh), F32),
            pltpu.VMEM((nb, dh, MOBA_BLOCK), BF16),
            pltpu.VMEM((nb, MOBA_BLOCK), F32),
        ],
        compiler_params=_params("parallel", "parallel"),
        name="moba",
    )(p, p, p)


def _merge_kernel(x_ref, lg_ref, lb_ref, ya_ref, yb_ref, ga_ref, gb_ref, wa_ref, wb_ref, wo_ref,
                  mg_ref, mb_ref, wr_ref, br_ref, h2_ref, logit_ref, *, alpha):
    h = _ln_rows(x_ref[...], lg_ref[...], lb_ref[...])
    a = jnp.dot(ya_ref[...], wa_ref[...], preferred_element_type=F32)
    b = jnp.dot(yb_ref[...], wb_ref[...], preferred_element_type=F32)
    merged = _sigmoid(ga_ref[...].astype(F32)) * a + _sigmoid(gb_ref[...].astype(F32)) * b
    z = jnp.dot(merged.astype(BF16), wo_ref[...], preferred_element_type=F32)
    h2 = _ln_rows(alpha * h + z, mg_ref[...], mb_ref[...])
    h2_ref[...] = h2
    logit_ref[...] = jnp.dot(h2.astype(BF16), wr_ref[...], preferred_element_type=F32) + br_ref[...]


def _merge(x2, ln_g, ln_b, y_a, y_b, p, ga_col, gb_col, w_a, w_b, w_o, mix_g, mix_b, w_r, b_r, alpha, tm):
    T, D = x2.shape
    row = lambda i: (i, 0)
    fixed = lambda i: (0, 0)
    return pl.pallas_call(
        functools.partial(_merge_kernel, alpha=alpha),
        grid=(T // tm,),
        in_specs=[
            pl.BlockSpec((tm, D), row),
            pl.BlockSpec((1, D), fixed),
            pl.BlockSpec((1, D), fixed),
            pl.BlockSpec((tm, D), row),
            pl.BlockSpec((tm, D), row),
            pl.BlockSpec((tm, D), lambda i: (i, ga_col)),
            pl.BlockSpec((tm, D), lambda i: (i, gb_col)),
            pl.BlockSpec((D, D), fixed),
            pl.BlockSpec((D, D), fixed),
            pl.BlockSpec((D, D), fixed),
            pl.BlockSpec((1, D), fixed),
            pl.BlockSpec((1, D), fixed),
            pl.BlockSpec((D, LANES), fixed),
            pl.BlockSpec((1, LANES), fixed),
        ],
        out_specs=[pl.BlockSpec((tm, D), row), pl.BlockSpec((tm, LANES), row)],
        out_shape=[jax.ShapeDtypeStruct((T, D), F32), jax.ShapeDtypeStruct((T, LANES), F32)],
        compiler_params=_params("parallel"),
        name="merge",
    )(x2, ln_g, ln_b, y_a, y_b, p, p, w_a, w_b, w_o, mix_g, mix_b, w_r, b_r)


META_ROWS = 2 * SUBLANES


def _route_kernel(logit_ref, dest_ref, gate_ref, meta_ref, cnt_scr, run_scr, start_scr,
                  *, n_exp, top_k, blk_rows):
    phase = pl.program_id(0)
    i = pl.program_id(1)
    tr = logit_ref.shape[0]
    lane = lax.broadcasted_iota(I32, (tr, LANES), 1)
    lane_f = lane.astype(F32)
    lg = jnp.where(lane < n_exp, logit_ref[...], -jnp.inf)
    hots, vals = [], []
    for _ in range(top_k):
        mk = jnp.max(lg, axis=1, keepdims=True)
        ik = jnp.min(jnp.where(lg == mk, lane_f, float(LANES)), axis=1, keepdims=True)
        hot = lane_f == ik
        hots.append(hot)
        vals.append(mk)
        lg = jnp.where(hot, -jnp.inf, lg)
    multi = jnp.zeros((tr, LANES), F32)
    for hot in hots:
        multi = multi + jnp.where(hot, 1.0, 0.0)
    colsum = jnp.sum(multi, axis=0, keepdims=True)

    @pl.when((phase == 0) & (i == 0))
    def _():
        cnt_scr[...] = jnp.zeros_like(cnt_scr)

    @pl.when(phase == 0)
    def _():
        cnt_scr[...] += colsum

    @pl.when((phase == 1) & (i == 0))
    def _():
        lane1 = lax.broadcasted_iota(I32, (1, LANES), 1)
        cnt = cnt_scr[...]
        padded = ((cnt.astype(I32) + (blk_rows - 1)) & (-blk_rows)).astype(F32)
        ends = padded
        shift = 1
        while shift < n_exp:
            ends = ends + jnp.where(lane1 >= shift, pltpu.roll(ends, shift, 1), 0.0)
            shift *= 2
        starts = ends - padded
        start_scr[...] = starts
        run_scr[...] = jnp.zeros_like(run_scr)
        sub = lax.broadcasted_iota(I32, (SUBLANES, LANES), 0)
        lane8 = lax.broadcasted_iota(I32, (SUBLANES, LANES), 1)
        first_row = ((sub * LANES + lane8) * blk_rows).astype(F32)
        owner = jnp.zeros((SUBLANES, LANES), I32)
        for e in range(n_exp):
            end_e = jnp.sum(jnp.where(lane1 == e, ends, 0.0), axis=1, keepdims=True)
            owner = owner + jnp.where(end_e <= first_row, 1, 0)
        meta_ref[...] = jnp.zeros_like(meta_ref)
        meta_ref[0:1, :] = cnt.astype(I32)
        meta_ref[1:2, :] = starts.astype(I32)
        meta_ref[2:3, :] = ends.astype(I32)
        meta_ref[SUBLANES:2 * SUBLANES, :] = jnp.minimum(owner, n_exp - 1)

    @pl.when(phase == 1)
    def _():
        r = lax.broadcasted_iota(I32, (tr, tr), 0)
        c = lax.broadcasted_iota(I32, (tr, tr), 1)
        before = jnp.where(c < r, 1.0, 0.0).astype(BF16)
        rank = jnp.dot(before, multi.astype(BF16), preferred_element_type=F32)
        pos = rank + (start_scr[...] + run_scr[...])
        denom = jnp.zeros((tr, 1), F32)
        exps = []
        for k in range(top_k):
            ek = jnp.exp(vals[k] - vals[0])
            exps.append(ek)
            denom = denom + ek
        dest = jnp.zeros((tr, LANES), F32)
        gate = jnp.zeros((tr, LANES), F32)
        for k in range(top_k):
            dk = jnp.sum(jnp.where(hots[k], pos, 0.0), axis=1, keepdims=True)
            dest = jnp.where(lane == k, dk, dest)
            gate = jnp.where(lane == k, exps[k] / denom, gate)
        dest_ref[...] = dest.astype(I32)
        gate_ref[...] = gate
        run_scr[...] += colsum


def _route(logits, n_exp, tr):
    T = logits.shape[0]
    tile = lambda ph, i: (i * ph, 0)
    return pl.pallas_call(
        functools.partial(_route_kernel, n_exp=n_exp, top_k=TOP_K, blk_rows=MOE_ROWS),
        grid=(2, T // tr),
        in_specs=[pl.BlockSpec((tr, LANES), lambda ph, i: (i, 0))],
        out_specs=[
            pl.BlockSpec((tr, LANES), tile),
            pl.BlockSpec((tr, LANES), tile),
            pl.BlockSpec((META_ROWS, LANES), lambda ph, i: (0, 0)),
        ],
        out_shape=[
            jax.ShapeDtypeStruct((T, LANES), I32),
            jax.ShapeDtypeStruct((T, LANES), F32),
            jax.ShapeDtypeStruct((META_ROWS, LANES), I32),
        ],
        scratch_shapes=[pltpu.VMEM((1, LANES), F32)] * 3,
        compiler_params=_params("arbitrary", "arbitrary"),
        name="route",
    )(logits)


def _dispatch_kernel(pad_start_ref, pad_cnt_ref, used_ref, dest_ref, h_ref, xs_ref, zero_scr, sem, zsem,
                     *, n_exp, n_blocks, top_k):
    i = pl.program_id(0)
    tr = h_ref.shape[0]
    blk_rows = zero_scr.shape[0]

    def zero_row(dst_row):
        return pltpu.make_async_copy(zero_scr.at[pl.ds(0, 1)], xs_ref.at[pl.ds(dst_row, 1)], zsem)

    def token_row(t, dst_row):
        return pltpu.make_async_copy(h_ref.at[pl.ds(t, 1)], xs_ref.at[pl.ds(dst_row, 1)], sem)

    @pl.when(i == 0)
    def _():
        zero_scr[...] = jnp.zeros_like(zero_scr)

        def per_expert(e, total):
            s = pad_start_ref[e]
            n = pad_cnt_ref[e]

            def per_row(r, c):
                zero_row(s + r).start()
                return c

            lax.fori_loop(0, n, per_row, 0)
            return total + n

        total = lax.fori_loop(0, n_exp, per_expert, 0)

        def wait_row(r, c):
            zero_row(0).wait()
            return c

        lax.fori_loop(0, total, wait_row, 0)

        def per_block(b, c):
            r0 = pl.multiple_of(b * blk_rows, blk_rows)
            cp = pltpu.make_async_copy(zero_scr, xs_ref.at[pl.ds(r0, blk_rows)], zsem)
            cp.start()
            cp.wait()
            return c

        lax.fori_loop(used_ref[0], n_blocks, per_block, 0)

    def send(t, c):
        for k in range(top_k):
            token_row(t, dest_ref[t * top_k + k]).start()
        return c

    lax.fori_loop(0, tr, send, 0)

    def drain(t, c):
        token_row(0, 0).wait()
        return c

    lax.fori_loop(0, tr * top_k, drain, 0)


def _dispatch(pad_start, pad_cnt, n_used, dest_flat, h2, n_blocks, tr):
    T, D = h2.shape
    n_exp = pad_start.shape[0]
    grid_spec = pltpu.PrefetchScalarGridSpec(
        num_scalar_prefetch=3,
        grid=(T // tr,),
        in_specs=[
            pl.BlockSpec((tr * TOP_K,), lambda i, *_: (i,), memory_space=pltpu.SMEM),
            pl.BlockSpec((tr, D), lambda i, *_: (i, 0)),
        ],
        out_specs=pl.BlockSpec(memory_space=pl.ANY),
        scratch_shapes=[
            pltpu.VMEM((MOE_ROWS, D), F32),
            pltpu.SemaphoreType.DMA(()),
            pltpu.SemaphoreType.DMA(()),
        ],
    )
    return pl.pallas_call(
        functools.partial(_dispatch_kernel, n_exp=n_exp, n_blocks=n_blocks, top_k=TOP_K),
        grid_spec=grid_spec,
        out_shape=jax.ShapeDtypeStruct((n_blocks * MOE_ROWS, D), F32),
        compiler_params=_params("arbitrary"),
        name="dispatch",
    )(pad_start, pad_cnt, n_used, dest_flat, h2)


def _expert_kernel(owner_ref, used_ref, x_ref, wu_ref, bu_ref, wd_ref, bd_ref, o_ref, *, d_ff):
    i = pl.program_id(0)

    @pl.when(i < used_ref[0])
    def _():
        x = x_ref[...].astype(BF16)
        gu = jnp.dot(x, wu_ref[0], preferred_element_type=F32) + bu_ref[0]
        g = jnp.minimum(gu[:, :d_ff], SWIGLU_LIMIT)
        u = jnp.clip(gu[:, d_ff:], -SWIGLU_LIMIT, SWIGLU_LIMIT)
        act = (u + 1.0) * (g * _sigmoid(SWIGLU_ALPHA * g))
        o_ref[...] = jnp.dot(act.astype(BF16), wd_ref[0], preferred_element_type=F32) + bd_ref[0]

    @pl.when(i >= used_ref[0])
    def _():
        o_ref[...] = jnp.zeros_like(o_ref)


def _experts(owner, n_used, xs, w_up, b_up, w_down, b_down):
    R, D = xs.shape
    n_exp, _, two_f = w_up.shape
    d_ff = two_f // 2
    n_blocks = R // MOE_ROWS
    grid_spec = pltpu.PrefetchScalarGridSpec(
        num_scalar_prefetch=2,
        grid=(n_blocks,),
        in_specs=[
            pl.BlockSpec((MOE_ROWS, D), lambda i, own, used: (jnp.minimum(i, used[0] - 1), 0)),
            pl.BlockSpec((1, D, two_f), lambda i, own, used: (own[i], 0, 0)),
            pl.BlockSpec((1, 1, two_f), lambda i, own, used: (own[i], 0, 0)),
            pl.BlockSpec((1, d_ff, D), lambda i, own, used: (own[i], 0, 0)),
            pl.BlockSpec((1, 1, D), lambda i, own, used: (own[i], 0, 0)),
        ],
        out_specs=pl.BlockSpec((MOE_ROWS, D), lambda i, own, used: (i, 0)),
    )
    return pl.pallas_call(
        functools.partial(_expert_kernel, d_ff=d_ff),
        grid_spec=grid_spec,
        out_shape=jax.ShapeDtypeStruct((R, D), F32),
        compiler_params=_params("arbitrary"),
        name="experts",
    )(owner, n_used, xs, w_up, b_up.reshape(n_exp, 1, two_f), w_down, b_down.reshape(n_exp, 1, D))


def _combine_kernel(dest_ref, gate_ref, h2_ref, g_ref, b_ref, y_ref, o_ref, ybuf, sem, *, alpha, top_k):
    tr = h2_ref.shape[0]

    def fetch(t, k, src_row):
        return pltpu.make_async_copy(y_ref.at[pl.ds(src_row, 1)], ybuf.at[k, pl.ds(t, 1)], sem)

    def start(t, c):
        for k in range(top_k):
            fetch(t, k, dest_ref[t * top_k + k]).start()
        return c

    lax.fori_loop(0, tr, start, 0)

    def drain(t, c):
        fetch(0, 0, 0).wait()
        return c

    lax.fori_loop(0, tr * top_k, drain, 0)

    gate = gate_ref[...]
    f = gate[:, 0:1] * ybuf[0]
    for k in range(1, top_k):
        f = f + gate[:, k:k + 1] * ybuf[k]
    o_ref[...] = _ln_rows(alpha * h2_ref[...] + f, g_ref[...], b_ref[...])


def _combine(dest_flat, gate, h2, ln_g, ln_b, y, alpha, tr):
    T, D = h2.shape
    return pl.pallas_call(
        functools.partial(_combine_kernel, alpha=alpha, top_k=TOP_K),
        grid=(T // tr,),
        in_specs=[
            pl.BlockSpec((tr * TOP_K,), lambda i: (i,), memory_space=pltpu.SMEM),
            pl.BlockSpec((tr, LANES), lambda i: (i, 0)),
            pl.BlockSpec((tr, D), lambda i: (i, 0)),
            pl.BlockSpec((1, D), lambda i: (0, 0)),
            pl.BlockSpec((1, D), lambda i: (0, 0)),
            pl.BlockSpec(memory_space=pl.ANY),
        ],
        out_specs=pl.BlockSpec((tr, D), lambda i: (i, 0)),
        out_shape=jax.ShapeDtypeStruct((T, D), F32),
        scratch_shapes=[pltpu.VMEM((TOP_K, tr, D), F32), pltpu.SemaphoreType.DMA(())],
        compiler_params=_params("arbitrary"),
        name="combine",
    )(dest_flat, gate, h2, ln_g, ln_b, y)


def _tile(n, want):
    t = min(n, want)
    assert n % t == 0, (n, want)
    return t


def kernel(x, ln_in_g, ln_in_b, w_in, gmlp_ln_g, gmlp_ln_b, w_spatial, b_spatial, w_branch_a, w_branch_b,
           w_out, ln_mix_g, ln_mix_b, w_router, b_router, w_up, b_up, w_down, b_down, ln_ffn_g, ln_ffn_b):
    B, S, D = x.shape
    depth = w_in.shape[0]
    assert depth == 1
    T = B * S
    n_exp = w_router.shape[-1]
    dh = D // N_HEADS
    alpha = (2 * depth) ** 0.25
    assert S % MOBA_BLOCK == 0 and T % ROUTE_ROWS == 0 and n_exp <= LANES
    row = lambda a: a.reshape(1, -1)

    x2 = x.reshape(T, D)
    p = _in_proj(x2, row(ln_in_g), row(ln_in_b), w_in[0].astype(BF16), _tile(T, 512), 1792)
    y_a = _mixer_a(p, row(gmlp_ln_g[0]), row(gmlp_ln_b[0]), w_spatial[0], b_spatial[0], D, _tile(T, 512))
    y_b = _moba(p, B, S, N_HEADS, dh, 2 * N_HEADS, 3 * N_HEADS, 4 * N_HEADS)

    w_r = jnp.zeros((D, LANES), BF16).at[:, :n_exp].set(w_router[0].astype(BF16))
    b_r = jnp.zeros((1, LANES), F32).at[0, :n_exp].set(b_router[0])
    h2, logits = _merge(x2, row(ln_in_g), row(ln_in_b), y_a, y_b, p, 5, 6,
                        w_branch_a[0].astype(BF16), w_branch_b[0].astype(BF16), w_out[0].astype(BF16),
                        row(ln_mix_g[0]), row(ln_mix_b[0]), w_r, b_r, alpha, _tile(T, 512))

    dest, gate, meta = _route(logits, n_exp, ROUTE_ROWS)
    n_blocks = -(-T * TOP_K // MOE_ROWS) + n_exp
    counts, starts, ends = meta[0, :n_exp], meta[1, :n_exp], meta[2, :n_exp]
    owner = meta[SUBLANES:].reshape(-1)[:n_blocks]
    n_used = (ends[n_exp - 1:] // MOE_ROWS).astype(I32)
    dest_flat = dest[:, :TOP_K].reshape(-1)

    xs = _dispatch(starts + counts, ends - starts - counts, n_used, dest_flat, h2, n_blocks, ROUTE_ROWS)
    y = _experts(owner, n_used, xs, w_up[0].astype(BF16), b_up[0], w_down[0].astype(BF16), b_down[0])
    out = _combine(dest_flat, gate, h2, row(ln_ffn_g[0]), row(ln_ffn_b[0]), y, alpha, ROUTE_ROWS)
    return out.reshape(B, S, D)
```

```python
import functools
import math

import jax
import jax.numpy as jnp
from jax import lax
from jax.experimental import pallas as pl
from jax.experimental.pallas import tpu as pltpu

F32 = jnp.float32
BF16 = jnp.bfloat16
I32 = jnp.int32

LN_EPS = 1e-5
N_HEADS = 8
MOBA_BLOCK = 256
MOBA_TOPK = 3
TOP_K = 4
SWIGLU_LIMIT = 7.0
SWIGLU_ALPHA = 1.702
LANES = 128
SUBLANES = 8
MOE_ROWS = 256
ROUTE_ROWS = 256
VMEM_LIMIT = 48 * 1024 * 1024


def _ln_rows(x, g, b):
    mu = jnp.mean(x, axis=-1, keepdims=True)
    xc = x - mu
    var = jnp.mean(xc * xc, axis=-1, keepdims=True)
    return xc * lax.rsqrt(var + LN_EPS) * g + b


def _gelu(x):
    return x * (lax.erf(x / math.sqrt(2.0)) + 1.0) / 2.0


def _sigmoid(x):
    return 1.0 / (1.0 + jnp.exp(-x))


def _params(*sem):
    return pltpu.CompilerParams(dimension_semantics=sem, vmem_limit_bytes=VMEM_LIMIT)


def _store_token_tiles(ref, x):
    rows, d = x.shape
    c = d // LANES
    for s in range(c):
        ref[pl.ds(s, rows, stride=c), :] = x[:, s * LANES:(s + 1) * LANES]


def _load_token_tiles(ref, rows, c):
    return jnp.concatenate([ref[pl.ds(s, rows, stride=c), :] for s in range(c)], axis=1)


def _inproj_kernel(x_ref, g_ref, b_ref, w_ref, o_ref, h_scr):
    @pl.when(pl.program_id(1) == 0)
    def _():
        h_scr[...] = _ln_rows(x_ref[...], g_ref[...], b_ref[...]).astype(BF16)

    o_ref[...] = jnp.dot(h_scr[...], w_ref[...], preferred_element_type=F32).astype(BF16)


def _in_proj(x2, ln_g, ln_b, w_in, tm, tn):
    T, D = x2.shape
    N = w_in.shape[1]
    return pl.pallas_call(
        _inproj_kernel,
        grid=(T // tm, N // tn),
        in_specs=[
            pl.BlockSpec((tm, D), lambda i, j: (i, 0)),
            pl.BlockSpec((1, D), lambda i, j: (0, 0)),
            pl.BlockSpec((1, D), lambda i, j: (0, 0)),
            pl.BlockSpec((D, tn), lambda i, j: (0, j)),
        ],
        out_specs=pl.BlockSpec((tm, tn), lambda i, j: (i, j)),
        out_shape=jax.ShapeDtypeStruct((T, N), BF16),
        scratch_shapes=[pltpu.VMEM((tm, D), BF16)],
        compiler_params=_params("parallel", "arbitrary"),
        name="in_proj",
    )(x2, ln_g, ln_b, w_in)


def _mixer_a_kernel(u_ref, v_ref, lng_ref, lnb_ref, ws_ref, bs_ref, o_ref, *, chunk, groups):
    tr, width = u_ref.shape
    gd = width // groups
    u = _gelu(u_ref[...].astype(F32))
    v = _gelu(v_ref[...].astype(F32))
    vn = _ln_rows(v, lng_ref[...], lnb_ref[...]).astype(BF16)
    row = lax.broadcasted_iota(I32, (chunk, chunk), 0)
    col = lax.broadcasted_iota(I32, (chunk, chunk), 1)
    causal = col <= row
    for g in range(groups):
        w = jnp.where(causal, ws_ref[g], 0.0).astype(BF16)
        for c in range(tr // chunk):
            rs = slice(c * chunk, (c + 1) * chunk)
            cs = slice(g * gd, (g + 1) * gd)
            vs = jnp.dot(w, vn[rs, cs], preferred_element_type=F32) + bs_ref[g]
            o_ref[rs, cs] = (u[rs, cs] * vs).astype(BF16)


def _mixer_a(p, ln_g, ln_b, w_s, b_s, width, tr):
    T = p.shape[0]
    groups, chunk, _ = w_s.shape
    gd = width // groups
    bias = jnp.broadcast_to(b_s[:, :, None], (groups, chunk, gd))
    return pl.pallas_call(
        functools.partial(_mixer_a_kernel, chunk=chunk, groups=groups),
        grid=(T // tr,),
        in_specs=[
            pl.BlockSpec((tr, width), lambda i: (i, 0)),
            pl.BlockSpec((tr, width), lambda i: (i, 1)),
            pl.BlockSpec((1, width), lambda i: (0, 0)),
            pl.BlockSpec((1, width), lambda i: (0, 0)),
            pl.BlockSpec((groups, chunk, chunk), lambda i: (0, 0, 0)),
            pl.BlockSpec((groups, chunk, gd), lambda i: (0, 0, 0)),
        ],
        out_specs=pl.BlockSpec((tr, width), lambda i: (i, 0)),
        out_shape=jax.ShapeDtypeStruct((T, width), BF16),
        compiler_params=_params("parallel"),
        name="mixer_a",
    )(p, p, ln_g, ln_b, w_s, bias)


_NT = (((1,), (1,)), ((), ()))


def _moba_kernel(q_ref, k_ref, v_ref, o_ref, kmean_scr, vt_scr, s_scr, p_scr, *, blk, topk, scale):
    S, dh = q_ref.shape
    nb = S // blk
    for n in range(nb):
        rs = slice(n * blk, (n + 1) * blk)
        kmean_scr[n:n + 1, :] = jnp.mean(k_ref[rs, :].astype(F32), axis=0, keepdims=True)
        vt_scr[:, rs] = v_ref[rs, :].astype(F32).T.astype(BF16)
    kmean = kmean_scr[...].astype(BF16)

    nidx = lax.broadcasted_iota(I32, (nb, blk), 0)
    kidx = lax.broadcasted_iota(I32, (blk, blk), 0)
    qidx = lax.broadcasted_iota(I32, (blk, blk), 1)
    causal = jnp.where(kidx <= qidx, 0.0, -jnp.inf)

    def fold(x, op):
        return op(x.reshape(blk // SUBLANES, SUBLANES, blk), axis=0)

    for qb in range(nb):
        q = q_ref[qb * blk:(qb + 1) * blk, :]
        keys = (qb + 1) * blk
        if qb > 0:
            gate = lax.dot_general(kmean, q, _NT, preferred_element_type=F32)
            past = nidx < qb
            gate = jnp.where(past, gate, -jnp.inf)
            rank = jnp.zeros((nb, blk), I32)
            for m in range(qb):
                gm = gate[m:m + 1, :]
                rank = rank + jnp.where(gm > gate, 1, jnp.where((gm == gate) & (nidx > m), 1, 0))
            bias = jnp.where(past & (rank < topk), 0.0, -jnp.inf)

        s_all = lax.dot_general(k_ref[0:keys, :], q, _NT, preferred_element_type=F32)
        mx = None
        for n in range(qb + 1):
            rs = slice(n * blk, (n + 1) * blk)
            s = s_all[rs, :] + (causal if n == qb else bias[n:n + 1, :])
            s_scr[rs, :] = s
            part = fold(s, jnp.max)
            mx = part if mx is None else jnp.maximum(mx, part)
        m_raw = jnp.max(mx, axis=0, keepdims=True)

        l8 = None
        for n in range(qb + 1):
            rs = slice(n * blk, (n + 1) * blk)
            p = jnp.exp((s_scr[rs, :] - m_raw) * scale)
            p_scr[rs, :] = p.astype(BF16)
            part = fold(p, jnp.sum)
            l8 = part if l8 is None else l8 + part
        l_fin = jnp.sum(l8, axis=0, keepdims=True)
        acc = jnp.dot(vt_scr[:, 0:keys], p_scr[0:keys, :], preferred_element_type=F32)
        o_ref[qb * blk:(qb + 1) * blk, :] = (acc * (1.0 / l_fin)).T.astype(BF16)


def _moba(p, B, S, n_heads, dh, q_col, k_col, v_col):
    T = p.shape[0]
    nb = S // MOBA_BLOCK
    kern = functools.partial(_moba_kernel, blk=MOBA_BLOCK, topk=MOBA_TOPK, scale=dh ** -0.5)
    return pl.pallas_call(
        kern,
        grid=(B, n_heads),
        in_specs=[
            pl.BlockSpec((S, dh), lambda b, h: (b, q_col + h)),
            pl.BlockSpec((S, dh), lambda b, h: (b, k_col + h)),
            pl.BlockSpec((S, dh), lambda b, h: (b, v_col + h)),
        ],
        out_specs=pl.BlockSpec((S, dh), lambda b, h: (b, h)),
        out_shape=jax.ShapeDtypeStruct((T, n_heads * dh), BF16),
        scratch_shapes=[
            pltpu.VMEM((nb, dh), F32),
            pltpu.VMEM((dh, S), BF16),
            pltpu.VMEM((S, MOBA_BLOCK), F32),
            pltpu.VMEM((S, MOBA_BLOCK), BF16),
        ],
        compiler_params=_params("parallel", "parallel"),
        name="moba",
    )(p, p, p)


def _merge_kernel(x_ref, lg_ref, lb_ref, ya_ref, yb_ref, ga_ref, gb_ref, wa_ref, wb_ref, wo_ref,
                  mg_ref, mb_ref, wr_ref, br_ref, h2_ref, logit_ref, *, alpha):
    h = _ln_rows(x_ref[...], lg_ref[...], lb_ref[...])
    a = jnp.dot(ya_ref[...], wa_ref[...], preferred_element_type=F32)
    b = jnp.dot(yb_ref[...], wb_ref[...], preferred_element_type=F32)
    merged = _sigmoid(ga_ref[...].astype(F32)) * a + _sigmoid(gb_ref[...].astype(F32)) * b
    z = jnp.dot(merged.astype(BF16), wo_ref[...], preferred_element_type=F32)
    h2 = _ln_rows(alpha * h + z, mg_ref[...], mb_ref[...])
    _store_token_tiles(h2_ref, h2)
    logit_ref[...] = jnp.dot(h2.astype(BF16), wr_ref[...], preferred_element_type=F32) + br_ref[...]


def _merge(x2, ln_g, ln_b, y_a, y_b, p, ga_col, gb_col, w_a, w_b, w_o, mix_g, mix_b, w_r, b_r, alpha, tm):
    T, D = x2.shape
    row = lambda i: (i, 0)
    fixed = lambda i: (0, 0)
    return pl.pallas_call(
        functools.partial(_merge_kernel, alpha=alpha),
        grid=(T // tm,),
        in_specs=[
            pl.BlockSpec((tm, D), row),
            pl.BlockSpec((1, D), fixed),
            pl.BlockSpec((1, D), fixed),
            pl.BlockSpec((tm, D), row),
            pl.BlockSpec((tm, D), row),
            pl.BlockSpec((tm, D), lambda i: (i, ga_col)),
            pl.BlockSpec((tm, D), lambda i: (i, gb_col)),
            pl.BlockSpec((D, D), fixed),
            pl.BlockSpec((D, D), fixed),
            pl.BlockSpec((D, D), fixed),
            pl.BlockSpec((1, D), fixed),
            pl.BlockSpec((1, D), fixed),
            pl.BlockSpec((D, LANES), fixed),
            pl.BlockSpec((1, LANES), fixed),
        ],
        out_specs=[pl.BlockSpec((tm * (D // LANES), LANES), row), pl.BlockSpec((tm, LANES), row)],
        out_shape=[jax.ShapeDtypeStruct((T * (D // LANES), LANES), F32),
                   jax.ShapeDtypeStruct((T, LANES), F32)],
        compiler_params=_params("parallel"),
        name="merge",
    )(x2, ln_g, ln_b, y_a, y_b, p, p, w_a, w_b, w_o, mix_g, mix_b, w_r, b_r)


META_ROWS = 2 * SUBLANES


def _route_kernel(logit_ref, dest_ref, gate_ref, meta_ref, cnt_scr, run_scr, start_scr,
                  *, n_exp, top_k, blk_rows):
    phase = pl.program_id(0)
    i = pl.program_id(1)
    tr = logit_ref.shape[0]
    lane = lax.broadcasted_iota(I32, (tr, LANES), 1)
    lane_f = lane.astype(F32)
    lg = jnp.where(lane < n_exp, logit_ref[...], -jnp.inf)
    hots, vals = [], []
    for _ in range(top_k):
        mk = jnp.max(lg, axis=1, keepdims=True)
        ik = jnp.min(jnp.where(lg == mk, lane_f, float(LANES)), axis=1, keepdims=True)
        hot = lane_f == ik
        hots.append(hot)
        vals.append(mk)
        lg = jnp.where(hot, -jnp.inf, lg)
    multi = jnp.zeros((tr, LANES), F32)
    for hot in hots:
        multi = multi + jnp.where(hot, 1.0, 0.0)
    colsum = jnp.sum(multi, axis=0, keepdims=True)

    @pl.when((phase == 0) & (i == 0))
    def _():
        cnt_scr[...] = jnp.zeros_like(cnt_scr)

    @pl.when(phase == 0)
    def _():
        cnt_scr[...] += colsum

    @pl.when((phase == 1) & (i == 0))
    def _():
        lane1 = lax.broadcasted_iota(I32, (1, LANES), 1)
        cnt = cnt_scr[...]
        padded = ((cnt.astype(I32) + (blk_rows - 1)) & (-blk_rows)).astype(F32)
        ends = padded
        shift = 1
        while shift < n_exp:
            ends = ends + jnp.where(lane1 >= shift, pltpu.roll(ends, shift, 1), 0.0)
            shift *= 2
        starts = ends - padded
        start_scr[...] = starts
        run_scr[...] = jnp.zeros_like(run_scr)
        sub = lax.broadcasted_iota(I32, (SUBLANES, LANES), 0)
        lane8 = lax.broadcasted_iota(I32, (SUBLANES, LANES), 1)
        first_row = ((sub * LANES + lane8) * blk_rows).astype(F32)
        owner = jnp.zeros((SUBLANES, LANES), I32)
        for e in range(n_exp):
            end_e = jnp.sum(jnp.where(lane1 == e, ends, 0.0), axis=1, keepdims=True)
            owner = owner + jnp.where(end_e <= first_row, 1, 0)
        meta_ref[...] = jnp.zeros_like(meta_ref)
        meta_ref[0:1, :] = cnt.astype(I32)
        meta_ref[1:2, :] = starts.astype(I32)
        meta_ref[2:3, :] = ends.astype(I32)
        meta_ref[SUBLANES:2 * SUBLANES, :] = jnp.minimum(owner, n_exp - 1)

    @pl.when(phase == 1)
    def _():
        r = lax.broadcasted_iota(I32, (tr, tr), 0)
        c = lax.broadcasted_iota(I32, (tr, tr), 1)
        before = jnp.where(c < r, 1.0, 0.0).astype(BF16)
        rank = jnp.dot(before, multi.astype(BF16), preferred_element_type=F32)
        pos = rank + (start_scr[...] + run_scr[...])
        denom = jnp.zeros((tr, 1), F32)
        exps = []
        for k in range(top_k):
            ek = jnp.exp(vals[k] - vals[0])
            exps.append(ek)
            denom = denom + ek
        dest = jnp.zeros((tr, LANES), F32)
        gate = jnp.zeros((tr, LANES), F32)
        for k in range(top_k):
            dk = jnp.sum(jnp.where(hots[k], pos, 0.0), axis=1, keepdims=True)
            dest = jnp.where(lane == k, dk, dest)
            gate = jnp.where(lane == k, exps[k] / denom, gate)
        dest_ref[...] = dest.astype(I32)
        gate_ref[...] = gate
        run_scr[...] += colsum


def _route(logits, n_exp, tr):
    T = logits.shape[0]
    tile = lambda ph, i: (i * ph, 0)
    return pl.pallas_call(
        functools.partial(_route_kernel, n_exp=n_exp, top_k=TOP_K, blk_rows=MOE_ROWS),
        grid=(2, T // tr),
        in_specs=[pl.BlockSpec((tr, LANES), lambda ph, i: (i, 0))],
        out_specs=[
            pl.BlockSpec((tr, LANES), tile),
            pl.BlockSpec((tr, LANES), tile),
            pl.BlockSpec((META_ROWS, LANES), lambda ph, i: (0, 0)),
        ],
        out_shape=[
            jax.ShapeDtypeStruct((T, LANES), I32),
            jax.ShapeDtypeStruct((T, LANES), F32),
            jax.ShapeDtypeStruct((META_ROWS, LANES), I32),
        ],
        scratch_shapes=[pltpu.VMEM((1, LANES), F32)] * 3,
        compiler_params=_params("arbitrary", "arbitrary"),
        name="route",
    )(logits)


def _dispatch_kernel(pad_start_ref, pad_cnt_ref, used_ref, dest_ref, h_ref, xs_ref, zero_scr, sem, zsem,
                     *, n_exp, n_blocks, top_k, c):
    i = pl.program_id(0)
    tr = h_ref.shape[0] // c
    blk_rows = zero_scr.shape[0]

    def tile(ref, r):
        return ref.at[pl.ds(pl.multiple_of(r * c, c), c)]

    def zero_row(dst_row):
        return pltpu.make_async_copy(tile(zero_scr, 0), tile(xs_ref, dst_row), zsem)

    def token_row(t, dst_row):
        return pltpu.make_async_copy(tile(h_ref, t), tile(xs_ref, dst_row), sem)

    @pl.when(i == 0)
    def _():
        zero_scr[...] = jnp.zeros_like(zero_scr)

        def per_expert(e, total):
            s = pad_start_ref[e]
            n = pad_cnt_ref[e]

            def per_row(r, c):
                zero_row(s + r).start()
                return c

            lax.fori_loop(0, n, per_row, 0)
            return total + n

        total = lax.fori_loop(0, n_exp, per_expert, 0)

        def wait_row(r, c):
            zero_row(0).wait()
            return c

        lax.fori_loop(0, total, wait_row, 0)

        def per_block(b, c):
            r0 = pl.multiple_of(b * blk_rows, blk_rows)
            cp = pltpu.make_async_copy(zero_scr, xs_ref.at[pl.ds(r0, blk_rows)], zsem)
            cp.start()
            cp.wait()
            return c

        lax.fori_loop(used_ref[0], n_blocks, per_block, 0)

    def send(t, c):
        for k in range(top_k):
            token_row(t, dest_ref[t * top_k + k]).start()
        return c

    lax.fori_loop(0, tr, send, 0)
    for _ in range(top_k):
        pltpu.make_async_copy(h_ref, xs_ref.at[pl.ds(0, tr * c)], sem).wait()


def _dispatch(pad_start, pad_cnt, n_used, dest_flat, h2_tiles, n_blocks, tr, c):
    T = h2_tiles.shape[0] // c
    n_exp = pad_start.shape[0]
    grid_spec = pltpu.PrefetchScalarGridSpec(
        num_scalar_prefetch=3,
        grid=(T // tr,),
        in_specs=[
            pl.BlockSpec((tr * TOP_K,), lambda i, *_: (i,), memory_space=pltpu.SMEM),
            pl.BlockSpec((tr * c, LANES), lambda i, *_: (i, 0)),
        ],
        out_specs=pl.BlockSpec(memory_space=pl.ANY),
        scratch_shapes=[
            pltpu.VMEM((MOE_ROWS * c, LANES), F32),
            pltpu.SemaphoreType.DMA(()),
            pltpu.SemaphoreType.DMA(()),
        ],
    )
    return pl.pallas_call(
        functools.partial(_dispatch_kernel, n_exp=n_exp, n_blocks=n_blocks, top_k=TOP_K, c=c),
        grid_spec=grid_spec,
        out_shape=jax.ShapeDtypeStruct((n_blocks * MOE_ROWS * c, LANES), F32),
        compiler_params=_params("arbitrary"),
        name="dispatch",
    )(pad_start, pad_cnt, n_used, dest_flat, h2_tiles)


def _expert_kernel(owner_ref, used_ref, x_ref, wu_ref, bu_ref, wd_ref, bd_ref, o_ref, *, d_ff, c):
    i = pl.program_id(0)
    rows = x_ref.shape[0] // c

    @pl.when(i < used_ref[0])
    def _():
        x = _load_token_tiles(x_ref, rows, c).astype(BF16)
        gu = jnp.dot(x, wu_ref[0], preferred_element_type=F32) + bu_ref[0]
        g = jnp.minimum(gu[:, :d_ff], SWIGLU_LIMIT)
        u = jnp.clip(gu[:, d_ff:], -SWIGLU_LIMIT, SWIGLU_LIMIT)
        act = (u + 1.0) * (g * _sigmoid(SWIGLU_ALPHA * g))
        y = jnp.dot(act.astype(BF16), wd_ref[0], preferred_element_type=F32) + bd_ref[0]
        _store_token_tiles(o_ref, y)

    @pl.when(i >= used_ref[0])
    def _():
        o_ref[...] = jnp.zeros_like(o_ref)


def _experts(owner, n_used, xs_tiles, w_up, b_up, w_down, b_down):
    n_exp, D, two_f = w_up.shape
    c = D // LANES
    d_ff = two_f // 2
    n_blocks = xs_tiles.shape[0] // (MOE_ROWS * c)
    grid_spec = pltpu.PrefetchScalarGridSpec(
        num_scalar_prefetch=2,
        grid=(n_blocks,),
        in_specs=[
            pl.BlockSpec((MOE_ROWS * c, LANES), lambda i, own, used: (jnp.minimum(i, used[0] - 1), 0)),
            pl.BlockSpec((1, D, two_f), lambda i, own, used: (own[i], 0, 0)),
            pl.BlockSpec((1, 1, two_f), lambda i, own, used: (own[i], 0, 0)),
            pl.BlockSpec((1, d_ff, D), lambda i, own, used: (own[i], 0, 0)),
            pl.BlockSpec((1, 1, D), lambda i, own, used: (own[i], 0, 0)),
        ],
        out_specs=pl.BlockSpec((MOE_ROWS * c, LANES), lambda i, own, used: (i, 0)),
    )
    return pl.pallas_call(
        functools.partial(_expert_kernel, d_ff=d_ff, c=c),
        grid_spec=grid_spec,
        out_shape=jax.ShapeDtypeStruct(xs_tiles.shape, F32),
        compiler_params=_params("arbitrary"),
        name="experts",
    )(owner, n_used, xs_tiles, w_up, b_up.reshape(n_exp, 1, two_f), w_down, b_down.reshape(n_exp, 1, D))


def _combine_kernel(dest_ref, gate_ref, h2_ref, g_ref, b_ref, y_ref, o_ref, ybuf, sem, *, alpha, top_k, c):
    tr = h2_ref.shape[0] // c

    def tile(ref, r):
        return ref.at[pl.ds(pl.multiple_of(r * c, c), c)]

    def fetch(t, k, src_row):
        return pltpu.make_async_copy(tile(y_ref, src_row), tile(ybuf.at[k], t), sem)

    def start(t, carry):
        for k in range(top_k):
            fetch(t, k, dest_ref[t * top_k + k]).start()
        return carry

    lax.fori_loop(0, tr, start, 0)
    for k in range(top_k):
        pltpu.make_async_copy(y_ref.at[pl.ds(0, tr * c)], ybuf.at[k], sem).wait()

    gate = gate_ref[...]
    f = alpha * _load_token_tiles(h2_ref, tr, c)
    for k in range(top_k):
        f = f + gate[:, k:k + 1] * _load_token_tiles(ybuf.at[k], tr, c)
    o_ref[...] = _ln_rows(f, g_ref[...], b_ref[...])


def _combine(dest_flat, gate, h2_tiles, ln_g, ln_b, y_tiles, alpha, tr, c):
    T = h2_tiles.shape[0] // c
    D = c * LANES
    return pl.pallas_call(
        functools.partial(_combine_kernel, alpha=alpha, top_k=TOP_K, c=c),
        grid=(T // tr,),
        in_specs=[
            pl.BlockSpec((tr * TOP_K,), lambda i: (i,), memory_space=pltpu.SMEM),
            pl.BlockSpec((tr, LANES), lambda i: (i, 0)),
            pl.BlockSpec((tr * c, LANES), lambda i: (i, 0)),
            pl.BlockSpec((1, D), lambda i: (0, 0)),
            pl.BlockSpec((1, D), lambda i: (0, 0)),
            pl.BlockSpec(memory_space=pl.ANY),
        ],
        out_specs=pl.BlockSpec((tr, D), lambda i: (i, 0)),
        out_shape=jax.ShapeDtypeStruct((T, D), F32),
        scratch_shapes=[pltpu.VMEM((TOP_K, tr * c, LANES), F32), pltpu.SemaphoreType.DMA(())],
        compiler_params=_params("arbitrary"),
        name="combine",
    )(dest_flat, gate, h2_tiles, ln_g, ln_b, y_tiles)


def _tile(n, want):
    t = min(n, want)
    assert n % t == 0, (n, want)
    return t


def kernel(x, ln_in_g, ln_in_b, w_in, gmlp_ln_g, gmlp_ln_b, w_spatial, b_spatial, w_branch_a, w_branch_b,
           w_out, ln_mix_g, ln_mix_b, w_router, b_router, w_up, b_up, w_down, b_down, ln_ffn_g, ln_ffn_b):
    B, S, D = x.shape
    depth = w_in.shape[0]
    assert depth == 1
    T = B * S
    n_exp = w_router.shape[-1]
    dh = D // N_HEADS
    alpha = (2 * depth) ** 0.25
    assert S % MOBA_BLOCK == 0 and T % ROUTE_ROWS == 0 and n_exp <= LANES
    row = lambda a: a.reshape(1, -1)

    x2 = x.reshape(T, D)
    p = _in_proj(x2, row(ln_in_g), row(ln_in_b), w_in[0].astype(BF16), _tile(T, 512), 1792)
    y_a = _mixer_a(p, row(gmlp_ln_g[0]), row(gmlp_ln_b[0]), w_spatial[0], b_spatial[0], D, _tile(T, 512))
    y_b = _moba(p, B, S, N_HEADS, dh, 2 * N_HEADS, 3 * N_HEADS, 4 * N_HEADS)

    w_r = jnp.zeros((D, LANES), BF16).at[:, :n_exp].set(w_router[0].astype(BF16))
    b_r = jnp.zeros((1, LANES), F32).at[0, :n_exp].set(b_router[0])
    h2, logits = _merge(x2, row(ln_in_g), row(ln_in_b), y_a, y_b, p, 5, 6,
                        w_branch_a[0].astype(BF16), w_branch_b[0].astype(BF16), w_out[0].astype(BF16),
                        row(ln_mix_g[0]), row(ln_mix_b[0]), w_r, b_r, alpha, _tile(T, 512))

    dest, gate, meta = _route(logits, n_exp, ROUTE_ROWS)
    n_blocks = -(-T * TOP_K // MOE_ROWS) + n_exp
    counts, starts, ends = meta[0, :n_exp], meta[1, :n_exp], meta[2, :n_exp]
    owner = meta[SUBLANES:].reshape(-1)[:n_blocks]
    n_used = (ends[n_exp - 1:] // MOE_ROWS).astype(I32)
    dest_flat = dest[:, :TOP_K].reshape(-1)

    c = D // LANES
    xs = _dispatch(starts + counts, ends - starts - counts, n_used, dest_flat, h2, n_blocks, ROUTE_ROWS, c)
    y = _experts(owner, n_used, xs, w_up[0].astype(BF16), b_up[0], w_down[0].astype(BF16), b_down[0])
    out = _combine(dest_flat, gate, h2, row(ln_ffn_g[0]), row(ln_ffn_b[0]), y, alpha, ROUTE_ROWS, c)
    return out.reshape(B, S, D)
```

```python
import functools
import math
from typing import NamedTuple

import jax
import jax.numpy as jnp
from jax import lax
from jax.experimental import pallas as pl
from jax.experimental.pallas import tpu as pltpu

F32 = jnp.float32
BF16 = jnp.bfloat16
I32 = jnp.int32

LN_EPS = 1e-5
LOG2_E = math.log2(math.e)
N_HEADS = 8
MOBA_BLOCK = 256
MOBA_TOPK = 3
MOBA_STREAMS = 4
TOP_K = 4
SWIGLU_LIMIT = 7.0
SWIGLU_ALPHA = 1.702
LANES = 128
SUBLANES = 8
MOE_ROWS = 256
ROUTE_ROWS = 256
SEND_GROUP = 4
COMBINE_PARTS = 2
VMEM_LIMIT = 48 * 1024 * 1024


def _ln_rows(x, g, b):
    mu = jnp.mean(x, axis=-1, keepdims=True)
    xc = x - mu
    var = jnp.mean(xc * xc, axis=-1, keepdims=True)
    return xc * lax.rsqrt(var + LN_EPS) * g + b


def _gelu(x):
    return x * (lax.erf(x / math.sqrt(2.0)) + 1.0) / 2.0


def _sigmoid(x):
    return 1.0 / (1.0 + jnp.exp(-x))


def _params(*sem):
    return pltpu.CompilerParams(dimension_semantics=sem, vmem_limit_bytes=VMEM_LIMIT)


def _store_token_tiles(ref, x):
    rows, d = x.shape
    c = d // LANES
    for s in range(c):
        ref[pl.ds(s, rows, stride=c), :] = x[:, s * LANES:(s + 1) * LANES]


def _load_token_tiles(ref, rows, c):
    return jnp.concatenate([ref[pl.ds(s, rows, stride=c), :] for s in range(c)], axis=1)


def _inproj_kernel(x_ref, g_ref, b_ref, w_ref, o_ref, *, tn):
    h = _ln_rows(x_ref[...], g_ref[...], b_ref[...]).astype(BF16)
    for j in range(w_ref.shape[1] // tn):
        cs = slice(j * tn, (j + 1) * tn)
        o_ref[:, cs] = jnp.dot(h, w_ref[:, cs], preferred_element_type=F32).astype(BF16)


def _in_proj(x2, ln_g, ln_b, w_in, tm, tn):
    T, D = x2.shape
    N = w_in.shape[1]
    assert N % tn == 0
    return pl.pallas_call(
        functools.partial(_inproj_kernel, tn=tn),
        grid=(T // tm,),
        in_specs=[
            pl.BlockSpec((tm, D), lambda i: (i, 0)),
            pl.BlockSpec((1, D), lambda i: (0, 0)),
            pl.BlockSpec((1, D), lambda i: (0, 0)),
            pl.BlockSpec((D, N), lambda i: (0, 0)),
        ],
        out_specs=pl.BlockSpec((tm, N), lambda i: (i, 0)),
        out_shape=jax.ShapeDtypeStruct((T, N), BF16),
        compiler_params=_params("parallel"),
        name="in_proj",
    )(x2, ln_g, ln_b, w_in)


def _mixer_a_kernel(u_ref, v_ref, lng_ref, lnb_ref, ws_ref, bs_ref, o_ref, *, chunk, groups):
    tr, width = u_ref.shape
    gd = width // groups
    u = _gelu(u_ref[...].astype(F32))
    v = _gelu(v_ref[...].astype(F32))
    vn = _ln_rows(v, lng_ref[...], lnb_ref[...]).astype(BF16)
    row = lax.broadcasted_iota(I32, (chunk, chunk), 0)
    col = lax.broadcasted_iota(I32, (chunk, chunk), 1)
    causal = col <= row
    for g in range(groups):
        w = jnp.where(causal, ws_ref[g], 0.0).astype(BF16)
        for c in range(tr // chunk):
            rs = slice(c * chunk, (c + 1) * chunk)
            cs = slice(g * gd, (g + 1) * gd)
            vs = jnp.dot(w, vn[rs, cs], preferred_element_type=F32) + bs_ref[g]
            o_ref[rs, cs] = (u[rs, cs] * vs).astype(BF16)


def _mixer_a(p, ln_g, ln_b, w_s, b_s, width, tr):
    T = p.shape[0]
    groups, chunk, _ = w_s.shape
    gd = width // groups
    bias = jnp.broadcast_to(b_s[:, :, None], (groups, chunk, gd))
    return pl.pallas_call(
        functools.partial(_mixer_a_kernel, chunk=chunk, groups=groups),
        grid=(T // tr,),
        in_specs=[
            pl.BlockSpec((tr, width), lambda i: (i, 0)),
            pl.BlockSpec((tr, width), lambda i: (i, 1)),
            pl.BlockSpec((1, width), lambda i: (0, 0)),
            pl.BlockSpec((1, width), lambda i: (0, 0)),
            pl.BlockSpec((groups, chunk, chunk), lambda i: (0, 0, 0)),
            pl.BlockSpec((groups, chunk, gd), lambda i: (0, 0, 0)),
        ],
        out_specs=pl.BlockSpec((tr, width), lambda i: (i, 0)),
        out_shape=jax.ShapeDtypeStruct((T, width), BF16),
        compiler_params=_params("parallel"),
        name="mixer_a",
    )(p, p, ln_g, ln_b, w_s, bias)


_NT = (((1,), (1,)), ((), ()))


def _moba_kernel(q_ref, k_ref, v_ref, o_ref, kmean_scr, vt_scr, s_scr, p_scr, *, blk, topk, scale):
    S, dh = q_ref.shape
    nb = S // blk
    for n in range(nb):
        rs = slice(n * blk, (n + 1) * blk)
        kmean_scr[n:n + 1, :] = jnp.mean(k_ref[rs, :].astype(F32), axis=0, keepdims=True)
        vt_scr[:, rs] = v_ref[rs, :].astype(F32).T.astype(BF16)
    kmean = kmean_scr[...].astype(BF16)

    nidx = lax.broadcasted_iota(I32, (nb, blk), 0)
    kidx = lax.broadcasted_iota(I32, (blk, blk), 0)
    qidx = lax.broadcasted_iota(I32, (blk, blk), 1)
    causal = jnp.where(kidx <= qidx, 0.0, -jnp.inf)

    def fold(x, op):
        return op(x.reshape(blk // SUBLANES, SUBLANES, blk), axis=0)

    def query_block(qb, s_scr, p_scr):
        q = q_ref[qb * blk:(qb + 1) * blk, :]
        keys = (qb + 1) * blk
        if qb > 0:
            gate = lax.dot_general(kmean, q, _NT, preferred_element_type=F32)
            past = nidx < qb
            gate = jnp.where(past, gate, -jnp.inf)
            rank = jnp.zeros((nb, blk), I32)
            for m in range(qb):
                gm = gate[m:m + 1, :]
                rank = rank + jnp.where(gm > gate, 1, jnp.where((gm == gate) & (nidx > m), 1, 0))
            bias = jnp.where(past & (rank < topk), 0.0, -jnp.inf)
        yield

        s_all = lax.dot_general(k_ref[0:keys, :], q, _NT, preferred_element_type=F32)
        yield
        mx = None
        for n in range(qb + 1):
            rs = slice(n * blk, (n + 1) * blk)
            s = s_all[rs, :] + (causal if n == qb else bias[n:n + 1, :])
            s_scr[rs, :] = s
            part = fold(s, jnp.max)
            mx = part if mx is None else jnp.maximum(mx, part)
        m_raw = jnp.max(mx, axis=0, keepdims=True)
        yield

        l8 = None
        for n in range(qb + 1):
            rs = slice(n * blk, (n + 1) * blk)
            p = jnp.exp2((s_scr[rs, :] - m_raw) * (scale * LOG2_E))
            p_scr[rs, :] = p.astype(BF16)
            part = fold(p, jnp.sum)
            l8 = part if l8 is None else l8 + part
        l_fin = jnp.sum(l8, axis=0, keepdims=True)
        yield
        acc = jnp.dot(vt_scr[:, 0:keys], p_scr[0:keys, :], preferred_element_type=F32)
        o_ref[qb * blk:(qb + 1) * blk, :] = (acc * (1.0 / l_fin)).T.astype(BF16)
        yield

    n_streams = s_scr.shape[0]
    order = []
    for a in range((nb + 1) // 2):
        order += [a] if a == nb - 1 - a else [a, nb - 1 - a]
    for g in range(0, nb, n_streams):
        streams = [query_block(qb, s_scr.at[i], p_scr.at[i]) for i, qb in enumerate(order[g:g + n_streams])]
        for _ in range(5):
            for st in streams:
                next(st)


def _moba(p, B, S, n_heads, dh, q_col, k_col, v_col):
    T = p.shape[0]
    nb = S // MOBA_BLOCK
    kern = functools.partial(_moba_kernel, blk=MOBA_BLOCK, topk=MOBA_TOPK, scale=dh ** -0.5)
    return pl.pallas_call(
        kern,
        grid=(B, n_heads),
        in_specs=[
            pl.BlockSpec((S, dh), lambda b, h: (b, q_col + h)),
            pl.BlockSpec((S, dh), lambda b, h: (b, k_col + h)),
            pl.BlockSpec((S, dh), lambda b, h: (b, v_col + h)),
        ],
        out_specs=pl.BlockSpec((S, dh), lambda b, h: (b, h)),
        out_shape=jax.ShapeDtypeStruct((T, n_heads * dh), BF16),
        scratch_shapes=[
            pltpu.VMEM((nb, dh), F32),
            pltpu.VMEM((dh, S), BF16),
            pltpu.VMEM((MOBA_STREAMS, S, MOBA_BLOCK), F32),
            pltpu.VMEM((MOBA_STREAMS, S, MOBA_BLOCK), BF16),
        ],
        compiler_params=_params("parallel", "parallel"),
        name="moba",
    )(p, p, p)


def _merge_kernel(x_ref, lg_ref, lb_ref, ya_ref, yb_ref, ga_ref, gb_ref, wa_ref, wb_ref, wo_ref,
                  mg_ref, mb_ref, wr_ref, br_ref, h2_ref, logit_ref, *, alpha):
    h = _ln_rows(x_ref[...], lg_ref[...], lb_ref[...])
    a = jnp.dot(ya_ref[...], wa_ref[...], preferred_element_type=F32)
    b = jnp.dot(yb_ref[...], wb_ref[...], preferred_element_type=F32)
    merged = _sigmoid(ga_ref[...].astype(F32)) * a + _sigmoid(gb_ref[...].astype(F32)) * b
    z = jnp.dot(merged.astype(BF16), wo_ref[...], preferred_element_type=F32)
    h2 = _ln_rows(alpha * h + z, mg_ref[...], mb_ref[...])
    _store_token_tiles(h2_ref, h2)
    logit_ref[...] = jnp.dot(h2.astype(BF16), wr_ref[...], preferred_element_type=F32) + br_ref[...]


def _merge(x2, ln_g, ln_b, y_a, y_b, p, ga_col, gb_col, w_a, w_b, w_o, mix_g, mix_b, w_r, b_r, alpha, tm):
    T, D = x2.shape
    row = lambda i: (i, 0)
    fixed = lambda i: (0, 0)
    return pl.pallas_call(
        functools.partial(_merge_kernel, alpha=alpha),
        grid=(T // tm,),
        in_specs=[
            pl.BlockSpec((tm, D), row),
            pl.BlockSpec((1, D), fixed),
            pl.BlockSpec((1, D), fixed),
            pl.BlockSpec((tm, D), row),
            pl.BlockSpec((tm, D), row),
            pl.BlockSpec((tm, D), lambda i: (i, ga_col)),
            pl.BlockSpec((tm, D), lambda i: (i, gb_col)),
            pl.BlockSpec((D, D), fixed),
            pl.BlockSpec((D, D), fixed),
            pl.BlockSpec((D, D), fixed),
            pl.BlockSpec((1, D), fixed),
            pl.BlockSpec((1, D), fixed),
            pl.BlockSpec((D, LANES), fixed),
            pl.BlockSpec((1, LANES), fixed),
        ],
        out_specs=[pl.BlockSpec((tm * (D // LANES), LANES), row), pl.BlockSpec((tm, LANES), row)],
        out_shape=[jax.ShapeDtypeStruct((T * (D // LANES), LANES), F32),
                   jax.ShapeDtypeStruct((T, LANES), F32)],
        compiler_params=_params("parallel"),
        name="merge",
    )(x2, ln_g, ln_b, y_a, y_b, p, p, w_a, w_b, w_o, mix_g, mix_b, w_r, b_r)


META_ROWS = 2 * SUBLANES


def _route_kernel(logit_ref, dest_ref, gate_ref, meta_ref, cnt_scr, run_scr, start_scr,
                  *, n_exp, top_k, blk_rows):
    phase = pl.program_id(0)
    i = pl.program_id(1)
    tr = logit_ref.shape[0]
    lane = lax.broadcasted_iota(I32, (tr, LANES), 1)
    lane_f = lane.astype(F32)
    lg = jnp.where(lane < n_exp, logit_ref[...], -jnp.inf)
    hots, vals = [], []
    for _ in range(top_k):
        mk = jnp.max(lg, axis=1, keepdims=True)
        ik = jnp.min(jnp.where(lg == mk, lane_f, float(LANES)), axis=1, keepdims=True)
        hot = lane_f == ik
        hots.append(hot)
        vals.append(mk)
        lg = jnp.where(hot, -jnp.inf, lg)
    multi = jnp.zeros((tr, LANES), F32)
    for hot in hots:
        multi = multi + jnp.where(hot, 1.0, 0.0)
    colsum = jnp.sum(multi, axis=0, keepdims=True)

    @pl.when((phase == 0) & (i == 0))
    def _():
        cnt_scr[...] = jnp.zeros_like(cnt_scr)

    @pl.when(phase == 0)
    def _():
        cnt_scr[...] += colsum

    @pl.when((phase == 1) & (i == 0))
    def _():
        lane1 = lax.broadcasted_iota(I32, (1, LANES), 1)
        cnt = cnt_scr[...]
        padded = ((cnt.astype(I32) + (blk_rows - 1)) & (-blk_rows)).astype(F32)
        ends = padded
        shift = 1
        while shift < n_exp:
            ends = ends + jnp.where(lane1 >= shift, pltpu.roll(ends, shift, 1), 0.0)
            shift *= 2
        starts = ends - padded
        start_scr[...] = starts
        run_scr[...] = jnp.zeros_like(run_scr)
        sub = lax.broadcasted_iota(I32, (SUBLANES, LANES), 0)
        lane8 = lax.broadcasted_iota(I32, (SUBLANES, LANES), 1)
        first_row = ((sub * LANES + lane8) * blk_rows).astype(F32)
        owner = jnp.zeros((SUBLANES, LANES), I32)
        for e in range(n_exp):
            end_e = jnp.sum(jnp.where(lane1 == e, ends, 0.0), axis=1, keepdims=True)
            owner = owner + jnp.where(end_e <= first_row, 1, 0)
        meta_ref[...] = jnp.zeros_like(meta_ref)
        meta_ref[0:1, :] = cnt.astype(I32)
        meta_ref[1:2, :] = starts.astype(I32)
        meta_ref[2:3, :] = ends.astype(I32)
        meta_ref[SUBLANES:2 * SUBLANES, :] = jnp.minimum(owner, n_exp - 1)

    @pl.when(phase == 1)
    def _():
        r = lax.broadcasted_iota(I32, (tr, tr), 0)
        c = lax.broadcasted_iota(I32, (tr, tr), 1)
        before = jnp.where(c < r, 1.0, 0.0).astype(BF16)
        rank = jnp.dot(before, multi.astype(BF16), preferred_element_type=F32)
        pos = rank + (start_scr[...] + run_scr[...])
        denom = jnp.zeros((tr, 1), F32)
        exps = []
        for k in range(top_k):
            ek = jnp.exp(vals[k] - vals[0])
            exps.append(ek)
            denom = denom + ek
        dest = jnp.zeros((tr, LANES), F32)
        gate = jnp.zeros((tr, LANES), F32)
        for k in range(top_k):
            dk = jnp.sum(jnp.where(hots[k], pos, 0.0), axis=1, keepdims=True)
            dest = jnp.where(lane == k, dk, dest)
            gate = jnp.where(lane == k, exps[k] / denom, gate)
        dest_ref[...] = dest.astype(I32)
        gate_ref[...] = gate
        run_scr[...] += colsum


def _route(logits, n_exp, tr):
    T = logits.shape[0]
    tile = lambda ph, i: (i * ph, 0)
    return pl.pallas_call(
        functools.partial(_route_kernel, n_exp=n_exp, top_k=TOP_K, blk_rows=MOE_ROWS),
        grid=(2, T // tr),
        in_specs=[pl.BlockSpec((tr, LANES), lambda ph, i: (i, 0))],
        out_specs=[
            pl.BlockSpec((tr, LANES), tile),
            pl.BlockSpec((tr, LANES), tile),
            pl.BlockSpec((META_ROWS, LANES), lambda ph, i: (0, 0)),
        ],
        out_shape=[
            jax.ShapeDtypeStruct((T, LANES), I32),
            jax.ShapeDtypeStruct((T, LANES), F32),
            jax.ShapeDtypeStruct((META_ROWS, LANES), I32),
        ],
        scratch_shapes=[pltpu.VMEM((1, LANES), F32)] * 3,
        compiler_params=_params("arbitrary", "arbitrary"),
        name="route",
    )(logits)


def _dispatch_kernel(pad_start_ref, pad_cnt_ref, used_ref, dest_ref, h_ref, xs_ref, zero_scr, sem, zsem,
                     *, n_exp, n_blocks, top_k, c):
    i = pl.program_id(0)
    tr = h_ref.shape[0] // c
    blk_rows = zero_scr.shape[0]

    def tile(ref, r):
        return ref.at[pl.ds(pl.multiple_of(r * c, c), c)]

    def zero_row(dst_row):
        return pltpu.make_async_copy(tile(zero_scr, 0), tile(xs_ref, dst_row), zsem)

    def token_row(t, dst_row):
        return pltpu.make_async_copy(tile(h_ref, t), tile(xs_ref, dst_row), sem)

    @pl.when(i == 0)
    def _():
        zero_scr[...] = jnp.zeros_like(zero_scr)

        def per_expert(e, total):
            s = pad_start_ref[e]
            n = pad_cnt_ref[e]

            def per_row(r, c):
                zero_row(s + r).start()
                return c

            lax.fori_loop(0, n, per_row, 0)
            return total + n

        total = lax.fori_loop(0, n_exp, per_expert, 0)

        def wait_row(r, c):
            zero_row(0).wait()
            return c

        lax.fori_loop(0, total, wait_row, 0)

        def per_block(b, c):
            r0 = pl.multiple_of(b * blk_rows, blk_rows)
            cp = pltpu.make_async_copy(zero_scr, xs_ref.at[pl.ds(r0, blk_rows)], zsem)
            cp.start()
            cp.wait()
            return c

        lax.fori_loop(used_ref[0], n_blocks, per_block, 0)

    def send(j, carry):
        t0 = j * SEND_GROUP
        rows = [dest_ref[(t0 + g) * top_k + k] for g in range(SEND_GROUP) for k in range(top_k)]
        for g in range(SEND_GROUP):
            for k in range(top_k):
                token_row(t0 + g, rows[g * top_k + k]).start()
        return carry

    lax.fori_loop(0, tr // SEND_GROUP, send, 0)
    for _ in range(top_k):
        pltpu.make_async_copy(h_ref, xs_ref.at[pl.ds(0, tr * c)], sem).wait()


def _dispatch(pad_start, pad_cnt, n_used, dest_flat, h2_tiles, n_blocks, tr, c):
    T = h2_tiles.shape[0] // c
    n_exp = pad_start.shape[0]
    grid_spec = pltpu.PrefetchScalarGridSpec(
        num_scalar_prefetch=3,
        grid=(T // tr,),
        in_specs=[
            pl.BlockSpec((tr * TOP_K,), lambda i, *_: (i,), memory_space=pltpu.SMEM),
            pl.BlockSpec((tr * c, LANES), lambda i, *_: (i, 0)),
        ],
        out_specs=pl.BlockSpec(memory_space=pl.ANY),
        scratch_shapes=[
            pltpu.VMEM((MOE_ROWS * c, LANES), F32),
            pltpu.SemaphoreType.DMA(()),
            pltpu.SemaphoreType.DMA(()),
        ],
    )
    return pl.pallas_call(
        functools.partial(_dispatch_kernel, n_exp=n_exp, n_blocks=n_blocks, top_k=TOP_K, c=c),
        grid_spec=grid_spec,
        out_shape=jax.ShapeDtypeStruct((n_blocks * MOE_ROWS * c, LANES), F32),
        compiler_params=_params("arbitrary"),
        name="dispatch",
    )(pad_start, pad_cnt, n_used, dest_flat, h2_tiles)


def _expert_kernel(owner_ref, used_ref, x_ref, wu_ref, bu_ref, wd_ref, bd_ref, o_ref, wu_scr, wd_scr, *, d_ff, c):
    i = pl.program_id(0)
    rows = x_ref.shape[0] // c
    used = i < used_ref[0]
    new_expert = (i == 0) | (owner_ref[i] != owner_ref[jnp.maximum(i - 1, 0)])

    @pl.when(used & new_expert)
    def _():
        wu_scr[...] = wu_ref[0].astype(BF16)
        wd_scr[...] = wd_ref[0].astype(BF16)

    @pl.when(used)
    def _():
        x = _load_token_tiles(x_ref, rows, c).astype(BF16)
        gu = jnp.dot(x, wu_scr[...], preferred_element_type=F32) + bu_ref[0]
        g = jnp.minimum(gu[:, :d_ff], SWIGLU_LIMIT)
        u = jnp.clip(gu[:, d_ff:], -SWIGLU_LIMIT, SWIGLU_LIMIT)
        act = (u + 1.0) * (g * _sigmoid(SWIGLU_ALPHA * g))
        y = jnp.dot(act.astype(BF16), wd_scr[...], preferred_element_type=F32) + bd_ref[0]
        _store_token_tiles(o_ref, y)

    @pl.when(i >= used_ref[0])
    def _():
        o_ref[...] = jnp.zeros_like(o_ref)


def _experts(owner, n_used, xs_tiles, w_up, b_up, w_down, b_down):
    n_exp, D, two_f = w_up.shape
    c = D // LANES
    d_ff = two_f // 2
    n_blocks = xs_tiles.shape[0] // (MOE_ROWS * c)
    grid_spec = pltpu.PrefetchScalarGridSpec(
        num_scalar_prefetch=2,
        grid=(n_blocks,),
        in_specs=[
            pl.BlockSpec((MOE_ROWS * c, LANES), lambda i, own, used: (jnp.minimum(i, used[0] - 1), 0)),
            pl.BlockSpec((1, D, two_f), lambda i, own, used: (own[i], 0, 0)),
            pl.BlockSpec((1, 1, two_f), lambda i, own, used: (own[i], 0, 0)),
            pl.BlockSpec((1, d_ff, D), lambda i, own, used: (own[i], 0, 0)),
            pl.BlockSpec((1, 1, D), lambda i, own, used: (own[i], 0, 0)),
        ],
        out_specs=pl.BlockSpec((MOE_ROWS * c, LANES), lambda i, own, used: (i, 0)),
        scratch_shapes=[pltpu.VMEM((D, two_f), BF16), pltpu.VMEM((d_ff, D), BF16)],
    )
    return pl.pallas_call(
        functools.partial(_expert_kernel, d_ff=d_ff, c=c),
        grid_spec=grid_spec,
        out_shape=jax.ShapeDtypeStruct(xs_tiles.shape, F32),
        compiler_params=_params("arbitrary"),
        name="experts",
    )(owner, n_used, xs_tiles, w_up, b_up.reshape(n_exp, 1, two_f), w_down, b_down.reshape(n_exp, 1, D))


def _combine_kernel(dest_ref, gate_ref, h2_ref, g_ref, b_ref, y_ref, o_ref, ybuf, sem, *, alpha, top_k, c, parts):
    tr = h2_ref.shape[0] // c
    pr = tr // parts

    def tile(ref, r):
        return ref.at[pl.ds(pl.multiple_of(r * c, c), c)]

    for part in range(parts):
        def start(j, carry, part=part):
            t0 = j * SEND_GROUP
            base = (part * pr + t0) * top_k
            rows = [dest_ref[base + g * top_k + k] for g in range(SEND_GROUP) for k in range(top_k)]
            for g in range(SEND_GROUP):
                for k in range(top_k):
                    pltpu.make_async_copy(tile(y_ref, rows[g * top_k + k]), tile(ybuf.at[part, k], t0 + g),
                                          sem.at[part]).start()
            return carry

        lax.fori_loop(0, pr // SEND_GROUP, start, 0)

    for part in range(parts):
        for k in range(top_k):
            pltpu.make_async_copy(y_ref.at[pl.ds(0, pr * c)], ybuf.at[part, k], sem.at[part]).wait()
        rs = slice(part * pr, (part + 1) * pr)
        gate = gate_ref[rs, :]
        f = alpha * _load_token_tiles(h2_ref.at[pl.ds(part * pr * c, pr * c)], pr, c)
        for k in range(top_k):
            f = f + gate[:, k:k + 1] * _load_token_tiles(ybuf.at[part, k], pr, c)
        o_ref[rs, :] = _ln_rows(f, g_ref[...], b_ref[...])


def _combine(dest_flat, gate, h2_tiles, ln_g, ln_b, y_tiles, alpha, tr, c):
    T = h2_tiles.shape[0] // c
    D = c * LANES
    return pl.pallas_call(
        functools.partial(_combine_kernel, alpha=alpha, top_k=TOP_K, c=c, parts=COMBINE_PARTS),
        grid=(T // tr,),
        in_specs=[
            pl.BlockSpec((tr * TOP_K,), lambda i: (i,), memory_space=pltpu.SMEM),
            pl.BlockSpec((tr, LANES), lambda i: (i, 0)),
            pl.BlockSpec((tr * c, LANES), lambda i: (i, 0)),
            pl.BlockSpec((1, D), lambda i: (0, 0)),
            pl.BlockSpec((1, D), lambda i: (0, 0)),
            pl.BlockSpec(memory_space=pl.ANY),
        ],
        out_specs=pl.BlockSpec((tr, D), lambda i: (i, 0)),
        out_shape=jax.ShapeDtypeStruct((T, D), F32),
        scratch_shapes=[pltpu.VMEM((COMBINE_PARTS, TOP_K, tr // COMBINE_PARTS * c, LANES), F32),
                        pltpu.SemaphoreType.DMA((COMBINE_PARTS,))],
        compiler_params=_params("arbitrary"),
        name="combine",
    )(dest_flat, gate, h2_tiles, ln_g, ln_b, y_tiles)


def _tile(n, want):
    t = min(n, want)
    assert n % t == 0, (n, want)
    return t


class _Tiles(NamedTuple):
    in_proj_rows: int
    in_proj_cols: int
    mixer_rows: int
    merge_rows: int
    route_rows: int
    move_rows: int


def _plan(T, n_cols):
    cols = n_cols // 4 if n_cols % (4 * 2 * LANES) == 0 else n_cols
    return _Tiles(_tile(T, 512), cols, _tile(T, 512), _tile(T, 512), _tile(T, 512), _tile(T, ROUTE_ROWS))


def kernel(x, ln_in_g, ln_in_b, w_in, gmlp_ln_g, gmlp_ln_b, w_spatial, b_spatial, w_branch_a, w_branch_b,
           w_out, ln_mix_g, ln_mix_b, w_router, b_router, w_up, b_up, w_down, b_down, ln_ffn_g, ln_ffn_b):
    B, S, D = x.shape
    depth = w_in.shape[0]
    assert depth == 1
    T = B * S
    n_exp = w_router.shape[-1]
    dh = D // N_HEADS
    alpha = (2 * depth) ** 0.25
    assert S % MOBA_BLOCK == 0 and T % ROUTE_ROWS == 0 and n_exp <= LANES
    row = lambda a: a.reshape(1, -1)

    x2 = x.reshape(T, D)
    tiles = _plan(T, w_in.shape[-1])
    p = _in_proj(x2, row(ln_in_g), row(ln_in_b), w_in[0].astype(BF16), tiles.in_proj_rows, tiles.in_proj_cols)
    y_a = _mixer_a(p, row(gmlp_ln_g[0]), row(gmlp_ln_b[0]), w_spatial[0], b_spatial[0], D, tiles.mixer_rows)
    y_b = _moba(p, B, S, N_HEADS, dh, 2 * N_HEADS, 3 * N_HEADS, 4 * N_HEADS)

    w_r = jnp.zeros((D, LANES), BF16).at[:, :n_exp].set(w_router[0].astype(BF16))
    b_r = jnp.zeros((1, LANES), F32).at[0, :n_exp].set(b_router[0])
    h2, logits = _merge(x2, row(ln_in_g), row(ln_in_b), y_a, y_b, p, 5, 6,
                        w_branch_a[0].astype(BF16), w_branch_b[0].astype(BF16), w_out[0].astype(BF16),
                        row(ln_mix_g[0]), row(ln_mix_b[0]), w_r, b_r, alpha, tiles.merge_rows)

    dest, gate, meta = _route(logits, n_exp, tiles.route_rows)
    n_blocks = -(-T * TOP_K // MOE_ROWS) + n_exp
    counts, starts, ends = meta[0, :n_exp], meta[1, :n_exp], meta[2, :n_exp]
    owner = meta[SUBLANES:].reshape(-1)[:n_blocks]
    n_used = (ends[n_exp - 1:] // MOE_ROWS).astype(I32)
    dest_flat = dest[:, :TOP_K].reshape(-1)

    c = D // LANES
    xs = _dispatch(starts + counts, ends - starts - counts, n_used, dest_flat, h2, n_blocks, tiles.move_rows, c)
    y = _experts(owner, n_used, xs, w_up[0], b_up[0], w_down[0], b_down[0])
    out = _combine(dest_flat, gate, h2, row(ln_ffn_g[0]), row(ln_ffn_b[0]), y, alpha, tiles.move_rows, c)
    return out.reshape(B, S, D)
```

```python
import functools
import math
from typing import NamedTuple

import jax
import jax.numpy as jnp
from jax import lax
from jax.experimental import pallas as pl
from jax.experimental.pallas import tpu as pltpu

F32 = jnp.float32
BF16 = jnp.bfloat16
I32 = jnp.int32

LN_EPS = 1e-5
LOG2_E = math.log2(math.e)
N_HEADS = 8
MOBA_BLOCK = 256
MOBA_TOPK = 3
MOBA_STREAMS = 4
TOP_K = 4
SWIGLU_LIMIT = 7.0
SWIGLU_ALPHA = 1.702
LANES = 128
SUBLANES = 8
MOE_ROWS = 512
ROUTE_ROWS = 256
SEND_GROUP = 4
COMBINE_PARTS = 2
EXPERT_PARTS = 2
VMEM_LIMIT = 48 * 1024 * 1024
VMEM_LIMIT_EXPERTS = 56 * 1024 * 1024


def _ln_rows(x, g, b):
    mu = jnp.mean(x, axis=-1, keepdims=True)
    xc = x - mu
    var = jnp.mean(xc * xc, axis=-1, keepdims=True)
    return xc * lax.rsqrt(var + LN_EPS) * g + b


def _gelu(x):
    return x * (lax.erf(x / math.sqrt(2.0)) + 1.0) / 2.0


def _sigmoid(x):
    return 1.0 / (1.0 + jnp.exp(-x))


def _params(*sem, vmem=VMEM_LIMIT):
    return pltpu.CompilerParams(dimension_semantics=sem, vmem_limit_bytes=vmem)


def _store_token_tiles(ref, x):
    rows, d = x.shape
    c = d // LANES
    for s in range(c):
        ref[pl.ds(s, rows, stride=c), :] = x[:, s * LANES:(s + 1) * LANES]


def _load_token_tiles(ref, rows, c):
    return jnp.concatenate([ref[pl.ds(s, rows, stride=c), :] for s in range(c)], axis=1)


def _inproj_kernel(x_ref, g_ref, b_ref, w_ref, o_ref, *, tn):
    h = _ln_rows(x_ref[...], g_ref[...], b_ref[...]).astype(BF16)
    for j in range(w_ref.shape[1] // tn):
        cs = slice(j * tn, (j + 1) * tn)
        o_ref[:, cs] = jnp.dot(h, w_ref[:, cs], preferred_element_type=F32).astype(BF16)


def _in_proj(x2, ln_g, ln_b, w_in, tm, tn):
    T, D = x2.shape
    N = w_in.shape[1]
    assert N % tn == 0
    return pl.pallas_call(
        functools.partial(_inproj_kernel, tn=tn),
        grid=(T // tm,),
        in_specs=[
            pl.BlockSpec((tm, D), lambda i: (i, 0)),
            pl.BlockSpec((1, D), lambda i: (0, 0)),
            pl.BlockSpec((1, D), lambda i: (0, 0)),
            pl.BlockSpec((D, N), lambda i: (0, 0)),
        ],
        out_specs=pl.BlockSpec((tm, N), lambda i: (i, 0)),
        out_shape=jax.ShapeDtypeStruct((T, N), BF16),
        compiler_params=_params("parallel"),
        name="in_proj",
    )(x2, ln_g, ln_b, w_in)


def _mixer_a_kernel(u_ref, v_ref, lng_ref, lnb_ref, ws_ref, bs_ref, o_ref, *, chunk, groups):
    tr, width = u_ref.shape
    gd = width // groups
    u = _gelu(u_ref[...].astype(F32))
    v = _gelu(v_ref[...].astype(F32))
    vn = _ln_rows(v, lng_ref[...], lnb_ref[...]).astype(BF16)
    row = lax.broadcasted_iota(I32, (chunk, chunk), 0)
    col = lax.broadcasted_iota(I32, (chunk, chunk), 1)
    causal = col <= row
    for g in range(groups):
        w = jnp.where(causal, ws_ref[g], 0.0).astype(BF16)
        for c in range(tr // chunk):
            rs = slice(c * chunk, (c + 1) * chunk)
            cs = slice(g * gd, (g + 1) * gd)
            vs = jnp.dot(w, vn[rs, cs], preferred_element_type=F32) + bs_ref[g]
            o_ref[rs, cs] = (u[rs, cs] * vs).astype(BF16)


def _mixer_a(p, ln_g, ln_b, w_s, b_s, width, tr):
    T = p.shape[0]
    groups, chunk, _ = w_s.shape
    gd = width // groups
    bias = jnp.broadcast_to(b_s[:, :, None], (groups, chunk, gd))
    return pl.pallas_call(
        functools.partial(_mixer_a_kernel, chunk=chunk, groups=groups),
        grid=(T // tr,),
        in_specs=[
            pl.BlockSpec((tr, width), lambda i: (i, 0)),
            pl.BlockSpec((tr, width), lambda i: (i, 1)),
            pl.BlockSpec((1, width), lambda i: (0, 0)),
            pl.BlockSpec((1, width), lambda i: (0, 0)),
            pl.BlockSpec((groups, chunk, chunk), lambda i: (0, 0, 0)),
            pl.BlockSpec((groups, chunk, gd), lambda i: (0, 0, 0)),
        ],
        out_specs=pl.BlockSpec((tr, width), lambda i: (i, 0)),
        out_shape=jax.ShapeDtypeStruct((T, width), BF16),
        compiler_params=_params("parallel"),
        name="mixer_a",
    )(p, p, ln_g, ln_b, w_s, bias)


_NT = (((1,), (1,)), ((), ()))


def _moba_kernel(q_ref, k_ref, v_ref, o_ref, kmean_scr, vt_scr, *stream_scr, blk, topk, scale):
    S, dh = q_ref.shape
    nb = S // blk
    for n in range(nb):
        rs = slice(n * blk, (n + 1) * blk)
        kmean_scr[n:n + 1, :] = jnp.mean(k_ref[rs, :].astype(F32), axis=0, keepdims=True)
        vt_scr[:, rs] = v_ref[rs, :].astype(F32).T.astype(BF16)
    kmean = kmean_scr[...].astype(BF16)

    nidx = lax.broadcasted_iota(I32, (nb, blk), 0)
    kidx = lax.broadcasted_iota(I32, (blk, blk), 0)
    qidx = lax.broadcasted_iota(I32, (blk, blk), 1)
    causal = jnp.where(kidx <= qidx, 0.0, -jnp.inf)

    def fold(x, op):
        return op(x.reshape(blk // SUBLANES, SUBLANES, blk), axis=0)

    def query_block(qb, s_scr, p_scr):
        q = q_ref[qb * blk:(qb + 1) * blk, :]
        keys = (qb + 1) * blk
        if qb > 0:
            gate = lax.dot_general(kmean, q, _NT, preferred_element_type=F32)
            past = nidx < qb
            gate = jnp.where(past, gate, -jnp.inf)
            rank = jnp.zeros((nb, blk), I32)
            for m in range(qb):
                gm = gate[m:m + 1, :]
                rank = rank + jnp.where(gm > gate, 1, jnp.where((gm == gate) & (nidx > m), 1, 0))
            bias = jnp.where(past & (rank < topk), 0.0, -jnp.inf)
        yield

        s_all = lax.dot_general(k_ref[0:keys, :], q, _NT, preferred_element_type=F32)
        yield
        mx = None
        for n in range(qb + 1):
            rs = slice(n * blk, (n + 1) * blk)
            s = s_all[rs, :] + (causal if n == qb else bias[n:n + 1, :])
            s_scr[rs, :] = s
            part = fold(s, jnp.max)
            mx = part if mx is None else jnp.maximum(mx, part)
        m_raw = jnp.max(mx, axis=0, keepdims=True)
        yield

        l8 = None
        for n in range(qb + 1):
            rs = slice(n * blk, (n + 1) * blk)
            p = jnp.exp2((s_scr[rs, :] - m_raw) * (scale * LOG2_E))
            p_scr[rs, :] = p.astype(BF16)
            part = fold(p, jnp.sum)
            l8 = part if l8 is None else l8 + part
        l_fin = jnp.sum(l8, axis=0, keepdims=True)
        yield
        acc = jnp.dot(vt_scr[:, 0:keys], p_scr[0:keys, :], preferred_element_type=F32)
        o_ref[qb * blk:(qb + 1) * blk, :] = (acc * (1.0 / l_fin)).T.astype(BF16)
        yield

    n_streams = len(stream_scr) // 2
    order = []
    for a in range((nb + 1) // 2):
        order += [a] if a == nb - 1 - a else [a, nb - 1 - a]
    for g in range(0, nb, n_streams):
        streams = [query_block(qb, stream_scr[2 * i], stream_scr[2 * i + 1])
                   for i, qb in enumerate(order[g:g + n_streams])]
        for _ in range(5):
            for st in streams:
                next(st)


def _moba(p, B, S, n_heads, dh, q_col, k_col, v_col):
    T = p.shape[0]
    nb = S // MOBA_BLOCK
    kern = functools.partial(_moba_kernel, blk=MOBA_BLOCK, topk=MOBA_TOPK, scale=dh ** -0.5)
    return pl.pallas_call(
        kern,
        grid=(B, n_heads),
        in_specs=[
            pl.BlockSpec((S, dh), lambda b, h: (b, q_col + h)),
            pl.BlockSpec((S, dh), lambda b, h: (b, k_col + h)),
            pl.BlockSpec((S, dh), lambda b, h: (b, v_col + h)),
        ],
        out_specs=pl.BlockSpec((S, dh), lambda b, h: (b, h)),
        out_shape=jax.ShapeDtypeStruct((T, n_heads * dh), BF16),
        scratch_shapes=[
            pltpu.VMEM((nb, dh), F32),
            pltpu.VMEM((dh, S), BF16),
        ] + [
            pltpu.VMEM((S, MOBA_BLOCK), F32),
            pltpu.VMEM((S, MOBA_BLOCK), BF16),
        ] * MOBA_STREAMS,
        compiler_params=_params("parallel", "parallel"),
        name="moba",
    )(p, p, p)


def _merge_kernel(x_ref, lg_ref, lb_ref, ya_ref, yb_ref, ga_ref, gb_ref, wa_ref, wb_ref, wo_ref,
                  mg_ref, mb_ref, wr_ref, br_ref, h2_ref, logit_ref, *, alpha):
    h = _ln_rows(x_ref[...], lg_ref[...], lb_ref[...])
    a = jnp.dot(ya_ref[...], wa_ref[...], preferred_element_type=F32)
    b = jnp.dot(yb_ref[...], wb_ref[...], preferred_element_type=F32)
    merged = _sigmoid(ga_ref[...].astype(F32)) * a + _sigmoid(gb_ref[...].astype(F32)) * b
    z = jnp.dot(merged.astype(BF16), wo_ref[...], preferred_element_type=F32)
    h2 = _ln_rows(alpha * h + z, mg_ref[...], mb_ref[...])
    _store_token_tiles(h2_ref, h2)
    logit_ref[...] = jnp.dot(h2.astype(BF16), wr_ref[...], preferred_element_type=F32) + br_ref[...]


def _merge(x2, ln_g, ln_b, y_a, y_b, p, ga_col, gb_col, w_a, w_b, w_o, mix_g, mix_b, w_r, b_r, alpha, tm):
    T, D = x2.shape
    row = lambda i: (i, 0)
    fixed = lambda i: (0, 0)
    return pl.pallas_call(
        functools.partial(_merge_kernel, alpha=alpha),
        grid=(T // tm,),
        in_specs=[
            pl.BlockSpec((tm, D), row),
            pl.BlockSpec((1, D), fixed),
            pl.BlockSpec((1, D), fixed),
            pl.BlockSpec((tm, D), row),
            pl.BlockSpec((tm, D), row),
            pl.BlockSpec((tm, D), lambda i: (i, ga_col)),
            pl.BlockSpec((tm, D), lambda i: (i, gb_col)),
            pl.BlockSpec((D, D), fixed),
            pl.BlockSpec((D, D), fixed),
            pl.BlockSpec((D, D), fixed),
            pl.BlockSpec((1, D), fixed),
            pl.BlockSpec((1, D), fixed),
            pl.BlockSpec((D, LANES), fixed),
            pl.BlockSpec((1, LANES), fixed),
        ],
        out_specs=[pl.BlockSpec((tm * (D // LANES), LANES), row), pl.BlockSpec((tm, LANES), row)],
        out_shape=[jax.ShapeDtypeStruct((T * (D // LANES), LANES), F32),
                   jax.ShapeDtypeStruct((T, LANES), F32)],
        compiler_params=_params("parallel"),
        name="merge",
    )(x2, ln_g, ln_b, y_a, y_b, p, p, w_a, w_b, w_o, mix_g, mix_b, w_r, b_r)


META_ROWS = 2 * SUBLANES


def _route_kernel(logit_ref, dest_ref, gate_ref, meta_ref, cnt_scr, run_scr, start_scr,
                  *, n_exp, top_k, blk_rows):
    phase = pl.program_id(0)
    i = pl.program_id(1)
    tr = logit_ref.shape[0]
    lane = lax.broadcasted_iota(I32, (tr, LANES), 1)
    lane_f = lane.astype(F32)
    lg = jnp.where(lane < n_exp, logit_ref[...], -jnp.inf)
    hots, vals = [], []
    for _ in range(top_k):
        mk = jnp.max(lg, axis=1, keepdims=True)
        ik = jnp.min(jnp.where(lg == mk, lane_f, float(LANES)), axis=1, keepdims=True)
        hot = lane_f == ik
        hots.append(hot)
        vals.append(mk)
        lg = jnp.where(hot, -jnp.inf, lg)
    multi = jnp.zeros((tr, LANES), F32)
    for hot in hots:
        multi = multi + jnp.where(hot, 1.0, 0.0)
    colsum = jnp.sum(multi, axis=0, keepdims=True)

    @pl.when((phase == 0) & (i == 0))
    def _():
        cnt_scr[...] = jnp.zeros_like(cnt_scr)

    @pl.when(phase == 0)
    def _():
        cnt_scr[...] += colsum

    @pl.when((phase == 1) & (i == 0))
    def _():
        lane1 = lax.broadcasted_iota(I32, (1, LANES), 1)
        cnt = cnt_scr[...]
        padded = ((cnt.astype(I32) + (blk_rows - 1)) & (-blk_rows)).astype(F32)
        ends = padded
        shift = 1
        while shift < n_exp:
            ends = ends + jnp.where(lane1 >= shift, pltpu.roll(ends, shift, 1), 0.0)
            shift *= 2
        starts = ends - padded
        start_scr[...] = starts
        run_scr[...] = jnp.zeros_like(run_scr)
        sub = lax.broadcasted_iota(I32, (SUBLANES, LANES), 0)
        lane8 = lax.broadcasted_iota(I32, (SUBLANES, LANES), 1)
        first_row = ((sub * LANES + lane8) * blk_rows).astype(F32)
        owner = jnp.zeros((SUBLANES, LANES), I32)
        for e in range(n_exp):
            end_e = jnp.sum(jnp.where(lane1 == e, ends, 0.0), axis=1, keepdims=True)
            owner = owner + jnp.where(end_e <= first_row, 1, 0)
        meta_ref[...] = jnp.zeros_like(meta_ref)
        meta_ref[0:1, :] = cnt.astype(I32)
        meta_ref[1:2, :] = starts.astype(I32)
        meta_ref[2:3, :] = ends.astype(I32)
        meta_ref[SUBLANES:2 * SUBLANES, :] = jnp.minimum(owner, n_exp - 1)

    @pl.when(phase == 1)
    def _():
        r = lax.broadcasted_iota(I32, (tr, tr), 0)
        c = lax.broadcasted_iota(I32, (tr, tr), 1)
        before = jnp.where(c < r, 1.0, 0.0).astype(BF16)
        rank = jnp.dot(before, multi.astype(BF16), preferred_element_type=F32)
        pos = rank + (start_scr[...] + run_scr[...])
        denom = jnp.zeros((tr, 1), F32)
        exps = []
        for k in range(top_k):
            ek = jnp.exp(vals[k] - vals[0])
            exps.append(ek)
            denom = denom + ek
        dest = jnp.zeros((tr, LANES), F32)
        gate = jnp.zeros((tr, LANES), F32)
        for k in range(top_k):
            dk = jnp.sum(jnp.where(hots[k], pos, 0.0), axis=1, keepdims=True)
            dest = jnp.where(lane == k, dk, dest)
            gate = jnp.where(lane == k, exps[k] / denom, gate)
        dest_ref[...] = dest.astype(I32)
        gate_ref[...] = gate
        run_scr[...] += colsum


def _route(logits, n_exp, tr):
    T = logits.shape[0]
    tile = lambda ph, i: (i * ph, 0)
    return pl.pallas_call(
        functools.partial(_route_kernel, n_exp=n_exp, top_k=TOP_K, blk_rows=MOE_ROWS),
        grid=(2, T // tr),
        in_specs=[pl.BlockSpec((tr, LANES), lambda ph, i: (i, 0))],
        out_specs=[
            pl.BlockSpec((tr, LANES), tile),
            pl.BlockSpec((tr, LANES), tile),
            pl.BlockSpec((META_ROWS, LANES), lambda ph, i: (0, 0)),
        ],
        out_shape=[
            jax.ShapeDtypeStruct((T, LANES), I32),
            jax.ShapeDtypeStruct((T, LANES), F32),
            jax.ShapeDtypeStruct((META_ROWS, LANES), I32),
        ],
        scratch_shapes=[pltpu.VMEM((1, LANES), F32)] * 3,
        compiler_params=_params("arbitrary", "arbitrary"),
        name="route",
    )(logits)


def _dispatch_kernel(pad_start_ref, pad_cnt_ref, used_ref, dest_ref, h_ref, xs_ref, zero_scr, sem, zsem,
                     *, n_exp, n_blocks, top_k, c):
    i = pl.program_id(0)
    tr = h_ref.shape[0] // c
    blk_rows = zero_scr.shape[0]

    def tile(ref, r):
        return ref.at[pl.ds(pl.multiple_of(r * c, c), c)]

    def zero_rows(dst_row, n_rows):
        dst = xs_ref.at[pl.ds(pl.multiple_of(dst_row * c, c), n_rows * c)]
        return pltpu.make_async_copy(zero_scr.at[pl.ds(0, n_rows * c)], dst, zsem)

    def zero_pad(e, wait):
        at = pad_start_ref[e]
        left = pad_cnt_ref[e]
        size = blk_rows // c // 2
        while size >= 1:
            has = (left & size) != 0

            @pl.when(has)
            def _(at=at, size=size):
                cp = zero_rows(at, size)
                cp.wait() if wait else cp.start()

            at = at + jnp.where(has, size, 0)
            size //= 2

    def token_row(t, dst_row):
        return pltpu.make_async_copy(tile(h_ref, t), tile(xs_ref, dst_row), sem)

    @pl.when(i == 0)
    def _():
        zero_scr[...] = jnp.zeros_like(zero_scr)

        def start_pad(e, carry):
            zero_pad(e, wait=False)
            return carry

        def wait_pad(e, carry):
            zero_pad(e, wait=True)
            return carry

        lax.fori_loop(0, n_exp, start_pad, 0)
        lax.fori_loop(0, n_exp, wait_pad, 0)

        def per_block(b, c):
            r0 = pl.multiple_of(b * blk_rows, blk_rows)
            cp = pltpu.make_async_copy(zero_scr, xs_ref.at[pl.ds(r0, blk_rows)], zsem)
            cp.start()
            cp.wait()
            return c

        lax.fori_loop(used_ref[0], n_blocks, per_block, 0)

    def send(j, carry):
        t0 = j * SEND_GROUP
        rows = [dest_ref[(t0 + g) * top_k + k] for g in range(SEND_GROUP) for k in range(top_k)]
        for g in range(SEND_GROUP):
            for k in range(top_k):
                token_row(t0 + g, rows[g * top_k + k]).start(priority=(g * top_k + k) % 2)
        return carry

    lax.fori_loop(0, tr // SEND_GROUP, send, 0)
    for _ in range(top_k):
        pltpu.make_async_copy(h_ref, xs_ref.at[pl.ds(0, tr * c)], sem).wait()


def _dispatch(pad_start, pad_cnt, n_used, dest_flat, h2_tiles, n_blocks, tr, c):
    T = h2_tiles.shape[0] // c
    n_exp = pad_start.shape[0]
    grid_spec = pltpu.PrefetchScalarGridSpec(
        num_scalar_prefetch=3,
        grid=(T // tr,),
        in_specs=[
            pl.BlockSpec((tr * TOP_K,), lambda i, *_: (i,), memory_space=pltpu.SMEM),
            pl.BlockSpec((tr * c, LANES), lambda i, *_: (i, 0)),
        ],
        out_specs=pl.BlockSpec(memory_space=pl.ANY),
        scratch_shapes=[
            pltpu.VMEM((MOE_ROWS * c, LANES), F32),
            pltpu.SemaphoreType.DMA(()),
            pltpu.SemaphoreType.DMA(()),
        ],
    )
    return pl.pallas_call(
        functools.partial(_dispatch_kernel, n_exp=n_exp, n_blocks=n_blocks, top_k=TOP_K, c=c),
        grid_spec=grid_spec,
        out_shape=jax.ShapeDtypeStruct((n_blocks * MOE_ROWS * c, LANES), F32),
        compiler_params=_params("arbitrary"),
        name="dispatch",
    )(pad_start, pad_cnt, n_used, dest_flat, h2_tiles)


def _expert_kernel(owner_ref, used_ref, x_ref, wu_ref, bu_ref, wd_ref, bd_ref, o_ref, wu_scr, wd_scr, *, d_ff, c):
    i = pl.program_id(0)
    rows = x_ref.shape[0] // c
    used = i < used_ref[0]
    new_expert = (i == 0) | (owner_ref[i] != owner_ref[jnp.maximum(i - 1, 0)])

    @pl.when(used & new_expert)
    def _():
        wu_scr[...] = wu_ref[0].astype(BF16)
        wd_scr[...] = wd_ref[0].astype(BF16)

    @pl.when(used)
    def _():
        pr = rows // EXPERT_PARTS
        for part in range(EXPERT_PARTS):
            rs = pl.ds(part * pr * c, pr * c)
            x = _load_token_tiles(x_ref.at[rs], pr, c).astype(BF16)
            gu = jnp.dot(x, wu_scr[...], preferred_element_type=F32) + bu_ref[0]
            g = jnp.minimum(gu[:, :d_ff], SWIGLU_LIMIT)
            u = jnp.clip(gu[:, d_ff:], -SWIGLU_LIMIT, SWIGLU_LIMIT)
            act = (u + 1.0) * (g * _sigmoid(SWIGLU_ALPHA * g))
            y = jnp.dot(act.astype(BF16), wd_scr[...], preferred_element_type=F32) + bd_ref[0]
            _store_token_tiles(o_ref.at[rs], y)

    @pl.when(i >= used_ref[0])
    def _():
        o_ref[...] = jnp.zeros_like(o_ref)


def _experts(owner, n_used, xs_tiles, w_up, b_up, w_down, b_down):
    n_exp, D, two_f = w_up.shape
    c = D // LANES
    d_ff = two_f // 2
    n_blocks = xs_tiles.shape[0] // (MOE_ROWS * c)
    grid_spec = pltpu.PrefetchScalarGridSpec(
        num_scalar_prefetch=2,
        grid=(n_blocks,),
        in_specs=[
            pl.BlockSpec((MOE_ROWS * c, LANES), lambda i, own, used: (jnp.minimum(i, used[0] - 1), 0)),
            pl.BlockSpec((1, D, two_f), lambda i, own, used: (own[i], 0, 0)),
            pl.BlockSpec((1, 1, two_f), lambda i, own, used: (own[i], 0, 0)),
            pl.BlockSpec((1, d_ff, D), lambda i, own, used: (own[i], 0, 0)),
            pl.BlockSpec((1, 1, D), lambda i, own, used: (own[i], 0, 0)),
        ],
        out_specs=pl.BlockSpec((MOE_ROWS * c, LANES), lambda i, own, used: (i, 0)),
        scratch_shapes=[pltpu.VMEM((D, two_f), BF16), pltpu.VMEM((d_ff, D), BF16)],
    )
    return pl.pallas_call(
        functools.partial(_expert_kernel, d_ff=d_ff, c=c),
        grid_spec=grid_spec,
        out_shape=jax.ShapeDtypeStruct(xs_tiles.shape, F32),
        compiler_params=_params("arbitrary", vmem=VMEM_LIMIT_EXPERTS),
        name="experts",
    )(owner, n_used, xs_tiles, w_up, b_up.reshape(n_exp, 1, two_f), w_down, b_down.reshape(n_exp, 1, D))


def _combine_kernel(dest_ref, gate_ref, h2_ref, g_ref, b_ref, y_ref, o_ref, ybuf, sem, *, alpha, top_k, c, parts):
    tr = h2_ref.shape[0] // c
    pr = tr // parts

    def tile(ref, r):
        return ref.at[pl.ds(pl.multiple_of(r * c, c), c)]

    for part in range(parts):
        def start(j, carry, part=part):
            t0 = j * SEND_GROUP
            base = (part * pr + t0) * top_k
            rows = [dest_ref[base + g * top_k + k] for g in range(SEND_GROUP) for k in range(top_k)]
            for g in range(SEND_GROUP):
                for k in range(top_k):
                    pltpu.make_async_copy(tile(y_ref, rows[g * top_k + k]), tile(ybuf.at[part, k], t0 + g),
                                          sem.at[part]).start(priority=(g * top_k + k) % 2)
            return carry

        lax.fori_loop(0, pr // SEND_GROUP, start, 0)

    for part in range(parts):
        for k in range(top_k):
            pltpu.make_async_copy(y_ref.at[pl.ds(0, pr * c)], ybuf.at[part, k], sem.at[part]).wait()
        rs = slice(part * pr, (part + 1) * pr)
        gate = gate_ref[rs, :]
        f = alpha * _load_token_tiles(h2_ref.at[pl.ds(part * pr * c, pr * c)], pr, c)
        for k in range(top_k):
            f = f + gate[:, k:k + 1] * _load_token_tiles(ybuf.at[part, k], pr, c)
        o_ref[rs, :] = _ln_rows(f, g_ref[...], b_ref[...])


def _combine(dest_flat, gate, h2_tiles, ln_g, ln_b, y_tiles, alpha, tr, c):
    T = h2_tiles.shape[0] // c
    D = c * LANES
    return pl.pallas_call(
        functools.partial(_combine_kernel, alpha=alpha, top_k=TOP_K, c=c, parts=COMBINE_PARTS),
        grid=(T // tr,),
        in_specs=[
            pl.BlockSpec((tr * TOP_K,), lambda i: (i,), memory_space=pltpu.SMEM),
            pl.BlockSpec((tr, LANES), lambda i: (i, 0)),
            pl.BlockSpec((tr * c, LANES), lambda i: (i, 0)),
            pl.BlockSpec((1, D), lambda i: (0, 0)),
            pl.BlockSpec((1, D), lambda i: (0, 0)),
            pl.BlockSpec(memory_space=pl.ANY),
        ],
        out_specs=pl.BlockSpec((tr, D), lambda i: (i, 0)),
        out_shape=jax.ShapeDtypeStruct((T, D), F32),
        scratch_shapes=[pltpu.VMEM((COMBINE_PARTS, TOP_K, tr // COMBINE_PARTS * c, LANES), F32),
                        pltpu.SemaphoreType.DMA((COMBINE_PARTS,))],
        compiler_params=_params("arbitrary"),
        name="combine",
    )(dest_flat, gate, h2_tiles, ln_g, ln_b, y_tiles)


def _tile(n, want):
    t = min(n, want)
    assert n % t == 0, (n, want)
    return t


class _Tiles(NamedTuple):
    in_proj_rows: int
    in_proj_cols: int
    mixer_rows: int
    merge_rows: int
    route_rows: int
    move_rows: int


def _plan(T, n_cols):
    cols = n_cols // 4 if n_cols % (4 * 2 * LANES) == 0 else n_cols
    return _Tiles(_tile(T, 512), cols, _tile(T, 512), _tile(T, 512), _tile(T, 512), _tile(T, ROUTE_ROWS))


def kernel(x, ln_in_g, ln_in_b, w_in, gmlp_ln_g, gmlp_ln_b, w_spatial, b_spatial, w_branch_a, w_branch_b,
           w_out, ln_mix_g, ln_mix_b, w_router, b_router, w_up, b_up, w_down, b_down, ln_ffn_g, ln_ffn_b):
    B, S, D = x.shape
    depth = w_in.shape[0]
    assert depth == 1
    T = B * S
    n_exp = w_router.shape[-1]
    dh = D // N_HEADS
    alpha = (2 * depth) ** 0.25
    assert S % MOBA_BLOCK == 0 and T % ROUTE_ROWS == 0 and n_exp <= LANES
    row = lambda a: a.reshape(1, -1)

    x2 = x.reshape(T, D)
    tiles = _plan(T, w_in.shape[-1])
    p = _in_proj(x2, row(ln_in_g), row(ln_in_b), w_in[0].astype(BF16), tiles.in_proj_rows, tiles.in_proj_cols)
    y_a = _mixer_a(p, row(gmlp_ln_g[0]), row(gmlp_ln_b[0]), w_spatial[0], b_spatial[0], D, tiles.mixer_rows)
    y_b = _moba(p, B, S, N_HEADS, dh, 2 * N_HEADS, 3 * N_HEADS, 4 * N_HEADS)

    w_r = jnp.zeros((D, LANES), BF16).at[:, :n_exp].set(w_router[0].astype(BF16))
    b_r = jnp.zeros((1, LANES), F32).at[0, :n_exp].set(b_router[0])
    h2, logits = _merge(x2, row(ln_in_g), row(ln_in_b), y_a, y_b, p, 5, 6,
                        w_branch_a[0].astype(BF16), w_branch_b[0].astype(BF16), w_out[0].astype(BF16),
                        row(ln_mix_g[0]), row(ln_mix_b[0]), w_r, b_r, alpha, tiles.merge_rows)

    dest, gate, meta = _route(logits, n_exp, tiles.route_rows)
    n_blocks = -(-T * TOP_K // MOE_ROWS) + n_exp
    counts, starts, ends = meta[0, :n_exp], meta[1, :n_exp], meta[2, :n_exp]
    owner = meta[SUBLANES:].reshape(-1)[:n_blocks]
    n_used = (ends[n_exp - 1:] // MOE_ROWS).astype(I32)
    dest_flat = dest[:, :TOP_K].reshape(-1)

    c = D // LANES
    xs = _dispatch(starts + counts, ends - starts - counts, n_used, dest_flat, h2, n_blocks, tiles.move_rows, c)
    y = _experts(owner, n_used, xs, w_up[0], b_up[0], w_down[0], b_down[0])
    out = _combine(dest_flat, gate, h2, row(ln_ffn_g[0]), row(ln_ffn_b[0]), y, alpha, tiles.move_rows, c)
    return out.reshape(B, S, D)
```

```python
import functools
import math
from typing import NamedTuple

import jax
import jax.numpy as jnp
from jax import lax
from jax.experimental import pallas as pl
from jax.experimental.pallas import tpu as pltpu

F32 = jnp.float32
BF16 = jnp.bfloat16
I32 = jnp.int32

LN_EPS = 1e-5
LOG2_E = math.log2(math.e)
N_HEADS = 8
MOBA_BLOCK = 256
MOBA_TOPK = 3
MOBA_STREAMS = 4
TOP_K = 4
SWIGLU_LIMIT = 7.0
SWIGLU_ALPHA = 1.702
LANES = 128
SUBLANES = 8
MOE_ROWS = 512
ROUTE_ROWS = 256
SEND_GROUP = 4
COMBINE_PARTS = 4
MERGE_PARTS = 2
EXPERT_PARTS = 2
VMEM_LIMIT = 48 * 1024 * 1024
VMEM_LIMIT_EXPERTS = 56 * 1024 * 1024


def _ln_rows(x, g, b):
    mu = jnp.mean(x, axis=-1, keepdims=True)
    xc = x - mu
    var = jnp.mean(xc * xc, axis=-1, keepdims=True)
    return xc * lax.rsqrt(var + LN_EPS) * g + b


def _gelu(x):
    return x * (lax.erf(x / math.sqrt(2.0)) + 1.0) / 2.0


def _sigmoid(x):
    return 1.0 / (1.0 + jnp.exp(-x))


def _params(*sem, vmem=VMEM_LIMIT):
    return pltpu.CompilerParams(dimension_semantics=sem, vmem_limit_bytes=vmem)


def _store_token_tiles(ref, x):
    rows, d = x.shape
    c = d // LANES
    for s in range(c):
        ref[pl.ds(s, rows, stride=c), :] = x[:, s * LANES:(s + 1) * LANES]


def _load_token_tiles(ref, rows, c):
    return jnp.concatenate([ref[pl.ds(s, rows, stride=c), :] for s in range(c)], axis=1)


def _inproj_kernel(x_ref, g_ref, b_ref, w_ref, o_ref, *, tn):
    h = _ln_rows(x_ref[...], g_ref[...], b_ref[...]).astype(BF16)
    for j in range(w_ref.shape[1] // tn):
        cs = slice(j * tn, (j + 1) * tn)
        o_ref[:, cs] = jnp.dot(h, w_ref[:, cs], preferred_element_type=F32).astype(BF16)


def _in_proj(x2, ln_g, ln_b, w_in, tm, tn):
    T, D = x2.shape
    N = w_in.shape[1]
    assert N % tn == 0
    return pl.pallas_call(
        functools.partial(_inproj_kernel, tn=tn),
        grid=(T // tm,),
        in_specs=[
            pl.BlockSpec((tm, D), lambda i: (i, 0)),
            pl.BlockSpec((1, D), lambda i: (0, 0)),
            pl.BlockSpec((1, D), lambda i: (0, 0)),
            pl.BlockSpec((D, N), lambda i: (0, 0)),
        ],
        out_specs=pl.BlockSpec((tm, N), lambda i: (i, 0)),
        out_shape=jax.ShapeDtypeStruct((T, N), BF16),
        compiler_params=_params("parallel"),
        name="in_proj",
    )(x2, ln_g, ln_b, w_in)


def _mixer_a_kernel(u_ref, v_ref, lng_ref, lnb_ref, ws_ref, bs_ref, o_ref, *, chunk, groups):
    tr, width = u_ref.shape
    gd = width // groups
    u = _gelu(u_ref[...].astype(F32))
    v = _gelu(v_ref[...].astype(F32))
    vn = _ln_rows(v, lng_ref[...], lnb_ref[...]).astype(BF16)
    row = lax.broadcasted_iota(I32, (chunk, chunk), 0)
    col = lax.broadcasted_iota(I32, (chunk, chunk), 1)
    causal = col <= row
    for g in range(groups):
        w = jnp.where(causal, ws_ref[g], 0.0).astype(BF16)
        for c in range(tr // chunk):
            rs = slice(c * chunk, (c + 1) * chunk)
            cs = slice(g * gd, (g + 1) * gd)
            vs = jnp.dot(w, vn[rs, cs], preferred_element_type=F32) + bs_ref[g]
            o_ref[rs, cs] = (u[rs, cs] * vs).astype(BF16)


def _mixer_a(p, ln_g, ln_b, w_s, b_s, width, tr):
    T = p.shape[0]
    groups, chunk, _ = w_s.shape
    gd = width // groups
    bias = jnp.broadcast_to(b_s[:, :, None], (groups, chunk, gd))
    return pl.pallas_call(
        functools.partial(_mixer_a_kernel, chunk=chunk, groups=groups),
        grid=(T // tr,),
        in_specs=[
            pl.BlockSpec((tr, width), lambda i: (i, 0)),
            pl.BlockSpec((tr, width), lambda i: (i, 1)),
            pl.BlockSpec((1, width), lambda i: (0, 0)),
            pl.BlockSpec((1, width), lambda i: (0, 0)),
            pl.BlockSpec((groups, chunk, chunk), lambda i: (0, 0, 0)),
            pl.BlockSpec((groups, chunk, gd), lambda i: (0, 0, 0)),
        ],
        out_specs=pl.BlockSpec((tr, width), lambda i: (i, 0)),
        out_shape=jax.ShapeDtypeStruct((T, width), BF16),
        compiler_params=_params("parallel"),
        name="mixer_a",
    )(p, p, ln_g, ln_b, w_s, bias)


_NT = (((1,), (1,)), ((), ()))


def _moba_kernel(q_ref, k_ref, v_ref, o_ref, kmean_scr, vt_scr, *stream_scr, blk, topk, scale):
    S, dh = q_ref.shape
    nb = S // blk
    for n in range(nb):
        rs = slice(n * blk, (n + 1) * blk)
        kmean_scr[n:n + 1, :] = jnp.mean(k_ref[rs, :].astype(F32), axis=0, keepdims=True)
        vt_scr[:, rs] = v_ref[rs, :].astype(F32).T.astype(BF16)
    kmean = kmean_scr[...].astype(BF16)

    nidx = lax.broadcasted_iota(I32, (nb, blk), 0)
    kidx = lax.broadcasted_iota(I32, (blk, blk), 0)
    qidx = lax.broadcasted_iota(I32, (blk, blk), 1)
    causal = jnp.where(kidx <= qidx, 0.0, -jnp.inf)

    def fold(x, op):
        return op(x.reshape(blk // SUBLANES, SUBLANES, blk), axis=0)

    def query_block(qb, s_scr, p_scr):
        q = q_ref[qb * blk:(qb + 1) * blk, :]
        keys = (qb + 1) * blk
        if qb > 0:
            gate = lax.dot_general(kmean, q, _NT, preferred_element_type=F32)
            past = nidx < qb
            gate = jnp.where(past, gate, -jnp.inf)
            rank = jnp.zeros((nb, blk), I32)
            for m in range(qb):
                gm = gate[m:m + 1, :]
                rank = rank + jnp.where(gm > gate, 1, jnp.where((gm == gate) & (nidx > m), 1, 0))
            bias = jnp.where(past & (rank < topk), 0.0, -jnp.inf)
        yield

        yield
        mx = None
        for n in range(qb + 1):
            rs = slice(n * blk, (n + 1) * blk)
            s = lax.dot_general(k_ref[rs, :], q, _NT, preferred_element_type=F32)
            s = s + (causal if n == qb else bias[n:n + 1, :])
            s_scr[rs, :] = s
            part = fold(s, jnp.max)
            mx = part if mx is None else jnp.maximum(mx, part)
        m_raw = jnp.max(mx, axis=0, keepdims=True)
        yield

        l8 = None
        for n in range(qb + 1):
            rs = slice(n * blk, (n + 1) * blk)
            p = jnp.exp2((s_scr[rs, :] - m_raw) * (scale * LOG2_E))
            p_scr[rs, :] = p.astype(BF16)
            part = fold(p, jnp.sum)
            l8 = part if l8 is None else l8 + part
        l_fin = jnp.sum(l8, axis=0, keepdims=True)
        yield
        acc = jnp.dot(vt_scr[:, 0:keys], p_scr[0:keys, :], preferred_element_type=F32)
        o_ref[qb * blk:(qb + 1) * blk, :] = (acc * (1.0 / l_fin)).T.astype(BF16)
        yield

    n_streams = len(stream_scr) // 2
    order = _moba_order(nb)
    for g in range(0, nb, n_streams):
        streams = [query_block(qb, stream_scr[2 * i], stream_scr[2 * i + 1])
                   for i, qb in enumerate(order[g:g + n_streams])]
        for _ in range(5):
            for st in streams:
                next(st)


def _moba_order(nb):
    order = []
    for a in range((nb + 1) // 2):
        order += [a] if a == nb - 1 - a else [a, nb - 1 - a]
    return order


def _moba(p, B, S, n_heads, dh, q_col, k_col, v_col):
    T = p.shape[0]
    nb = S // MOBA_BLOCK
    kern = functools.partial(_moba_kernel, blk=MOBA_BLOCK, topk=MOBA_TOPK, scale=dh ** -0.5)
    order = _moba_order(nb)
    n_streams = min(MOBA_STREAMS, nb)
    slot_keys = [max(order[i::n_streams]) * MOBA_BLOCK + MOBA_BLOCK for i in range(n_streams)]
    stream_scratch = []
    for keys in slot_keys:
        stream_scratch += [pltpu.VMEM((keys, MOBA_BLOCK), F32),
                           pltpu.VMEM((keys, MOBA_BLOCK), BF16)]
    return pl.pallas_call(
        kern,
        grid=(B, n_heads),
        in_specs=[
            pl.BlockSpec((S, dh), lambda b, h: (b, q_col + h)),
            pl.BlockSpec((S, dh), lambda b, h: (b, k_col + h)),
            pl.BlockSpec((S, dh), lambda b, h: (b, v_col + h)),
        ],
        out_specs=pl.BlockSpec((S, dh), lambda b, h: (b, h)),
        out_shape=jax.ShapeDtypeStruct((T, n_heads * dh), BF16),
        scratch_shapes=[
            pltpu.VMEM((nb, dh), F32),
            pltpu.VMEM((dh, S), BF16),
        ] + stream_scratch,
        compiler_params=_params("parallel", "parallel"),
        name="moba",
    )(p, p, p)


def _merge_kernel(x_ref, lg_ref, lb_ref, ya_ref, yb_ref, ga_ref, gb_ref, wa_ref, wb_ref, wo_ref,
                  mg_ref, mb_ref, wr_ref, br_ref, h2_ref, logit_ref, *, alpha):
    tm, d = x_ref.shape
    c = d // LANES
    pr = tm // MERGE_PARTS
    for part in range(MERGE_PARTS):
        rs = slice(part * pr, (part + 1) * pr)
        h = _ln_rows(x_ref[rs, :], lg_ref[...], lb_ref[...])
        a = jnp.dot(ya_ref[rs, :], wa_ref[...], preferred_element_type=F32)
        b = jnp.dot(yb_ref[rs, :], wb_ref[...], preferred_element_type=F32)
        merged = _sigmoid(ga_ref[rs, :].astype(F32)) * a + _sigmoid(gb_ref[rs, :].astype(F32)) * b
        z = jnp.dot(merged.astype(BF16), wo_ref[...], preferred_element_type=F32)
        h2 = _ln_rows(alpha * h + z, mg_ref[...], mb_ref[...])
        _store_token_tiles(h2_ref.at[pl.ds(part * pr * c, pr * c)], h2)
        logit_ref[rs, :] = jnp.dot(h2.astype(BF16), wr_ref[...], preferred_element_type=F32) + br_ref[...]


def _merge(x2, ln_g, ln_b, y_a, y_b, p, ga_col, gb_col, w_a, w_b, w_o, mix_g, mix_b, w_r, b_r, alpha, tm):
    T, D = x2.shape
    row = lambda i: (i, 0)
    fixed = lambda i: (0, 0)
    return pl.pallas_call(
        functools.partial(_merge_kernel, alpha=alpha),
        grid=(T // tm,),
        in_specs=[
            pl.BlockSpec((tm, D), row),
            pl.BlockSpec((1, D), fixed),
            pl.BlockSpec((1, D), fixed),
            pl.BlockSpec((tm, D), row),
            pl.BlockSpec((tm, D), row),
            pl.BlockSpec((tm, D), lambda i: (i, ga_col)),
            pl.BlockSpec((tm, D), lambda i: (i, gb_col)),
            pl.BlockSpec((D, D), fixed),
            pl.BlockSpec((D, D), fixed),
            pl.BlockSpec((D, D), fixed),
            pl.BlockSpec((1, D), fixed),
            pl.BlockSpec((1, D), fixed),
            pl.BlockSpec((D, LANES), fixed),
            pl.BlockSpec((1, LANES), fixed),
        ],
        out_specs=[pl.BlockSpec((tm * (D // LANES), LANES), row), pl.BlockSpec((tm, LANES), row)],
        out_shape=[jax.ShapeDtypeStruct((T * (D // LANES), LANES), F32),
                   jax.ShapeDtypeStruct((T, LANES), F32)],
        compiler_params=_params("parallel"),
        name="merge",
    )(x2, ln_g, ln_b, y_a, y_b, p, p, w_a, w_b, w_o, mix_g, mix_b, w_r, b_r)


META_ROWS = 2 * SUBLANES


def _route_kernel(logit_ref, dest_ref, gate_ref, meta_ref, cnt_scr, run_scr, start_scr,
                  *, n_exp, top_k, blk_rows):
    phase = pl.program_id(0)
    i = pl.program_id(1)
    tr = logit_ref.shape[0]
    lane = lax.broadcasted_iota(I32, (tr, LANES), 1)
    lane_f = lane.astype(F32)
    lg = jnp.where(lane < n_exp, logit_ref[...], -jnp.inf)
    hots, vals = [], []
    for _ in range(top_k):
        mk = jnp.max(lg, axis=1, keepdims=True)
        ik = jnp.min(jnp.where(lg == mk, lane_f, float(LANES)), axis=1, keepdims=True)
        hot = lane_f == ik
        hots.append(hot)
        vals.append(mk)
        lg = jnp.where(hot, -jnp.inf, lg)
    multi = jnp.zeros((tr, LANES), F32)
    for hot in hots:
        multi = multi + jnp.where(hot, 1.0, 0.0)
    colsum = jnp.sum(multi, axis=0, keepdims=True)

    @pl.when((phase == 0) & (i == 0))
    def _():
        cnt_scr[...] = jnp.zeros_like(cnt_scr)

    @pl.when(phase == 0)
    def _():
        cnt_scr[...] += colsum

    @pl.when((phase == 1) & (i == 0))
    def _():
        lane1 = lax.broadcasted_iota(I32, (1, LANES), 1)
        cnt = cnt_scr[...]
        padded = ((cnt.astype(I32) + (blk_rows - 1)) & (-blk_rows)).astype(F32)
        ends = padded
        shift = 1
        while shift < n_exp:
            ends = ends + jnp.where(lane1 >= shift, pltpu.roll(ends, shift, 1), 0.0)
            shift *= 2
        starts = ends - padded
        start_scr[...] = starts
        run_scr[...] = jnp.zeros_like(run_scr)
        sub = lax.broadcasted_iota(I32, (SUBLANES, LANES), 0)
        lane8 = lax.broadcasted_iota(I32, (SUBLANES, LANES), 1)
        first_row = ((sub * LANES + lane8) * blk_rows).astype(F32)
        owner = jnp.zeros((SUBLANES, LANES), I32)
        for e in range(n_exp):
            end_e = jnp.sum(jnp.where(lane1 == e, ends, 0.0), axis=1, keepdims=True)
            owner = owner + jnp.where(end_e <= first_row, 1, 0)
        meta_ref[...] = jnp.zeros_like(meta_ref)
        meta_ref[0:1, :] = cnt.astype(I32)
        meta_ref[1:2, :] = starts.astype(I32)
        meta_ref[2:3, :] = ends.astype(I32)
        meta_ref[SUBLANES:2 * SUBLANES, :] = jnp.minimum(owner, n_exp - 1)

    @pl.when(phase == 1)
    def _():
        r = lax.broadcasted_iota(I32, (tr, tr), 0)
        c = lax.broadcasted_iota(I32, (tr, tr), 1)
        before = jnp.where(c < r, 1.0, 0.0).astype(BF16)
        rank = jnp.dot(before, multi.astype(BF16), preferred_element_type=F32)
        pos = rank + (start_scr[...] + run_scr[...])
        denom = jnp.zeros((tr, 1), F32)
        exps = []
        for k in range(top_k):
            ek = jnp.exp(vals[k] - vals[0])
            exps.append(ek)
            denom = denom + ek
        dest = jnp.zeros((tr, LANES), F32)
        gate = jnp.zeros((tr, LANES), F32)
        for k in range(top_k):
            dk = jnp.sum(jnp.where(hots[k], pos, 0.0), axis=1, keepdims=True)
            dest = jnp.where(lane == k, dk, dest)
            gate = jnp.where(lane == k, exps[k] / denom, gate)
        dest_ref[...] = dest.astype(I32)
        gate_ref[...] = gate
        run_scr[...] += colsum


def _route(logits, n_exp, tr):
    T = logits.shape[0]
    tile = lambda ph, i: (i * ph, 0)
    return pl.pallas_call(
        functools.partial(_route_kernel, n_exp=n_exp, top_k=TOP_K, blk_rows=MOE_ROWS),
        grid=(2, T // tr),
        in_specs=[pl.BlockSpec((tr, LANES), lambda ph, i: (i, 0))],
        out_specs=[
            pl.BlockSpec((tr, LANES), tile),
            pl.BlockSpec((tr, LANES), tile),
            pl.BlockSpec((META_ROWS, LANES), lambda ph, i: (0, 0)),
        ],
        out_shape=[
            jax.ShapeDtypeStruct((T, LANES), I32),
            jax.ShapeDtypeStruct((T, LANES), F32),
            jax.ShapeDtypeStruct((META_ROWS, LANES), I32),
        ],
        scratch_shapes=[pltpu.VMEM((1, LANES), F32)] * 3,
        compiler_params=_params("arbitrary", "arbitrary"),
        name="route",
    )(logits)


def _dispatch_kernel(pad_start_ref, pad_cnt_ref, used_ref, dest_ref, h_ref, xs_ref, zero_scr, sem, zsem,
                     *, n_exp, n_blocks, top_k, c):
    i = pl.program_id(0)
    tr = h_ref.shape[0] // c
    blk_rows = zero_scr.shape[0]

    def tile(ref, r):
        return ref.at[pl.ds(pl.multiple_of(r * c, c), c)]

    def zero_rows(dst_row, n_rows):
        dst = xs_ref.at[pl.ds(pl.multiple_of(dst_row * c, c), n_rows * c)]
        return pltpu.make_async_copy(zero_scr.at[pl.ds(0, n_rows * c)], dst, zsem)

    def zero_pad(e, wait):
        at = pad_start_ref[e]
        left = pad_cnt_ref[e]
        size = blk_rows // c // 2
        while size >= 1:
            has = (left & size) != 0

            @pl.when(has)
            def _(at=at, size=size):
                cp = zero_rows(at, size)
                cp.wait() if wait else cp.start()

            at = at + jnp.where(has, size, 0)
            size //= 2

    def token_row(t, dst_row):
        return pltpu.make_async_copy(tile(h_ref, t), tile(xs_ref, dst_row), sem)

    @pl.when(i == 0)
    def _():
        zero_scr[...] = jnp.zeros_like(zero_scr)

        def start_pad(e, carry):
            zero_pad(e, wait=False)
            return carry

        def wait_pad(e, carry):
            zero_pad(e, wait=True)
            return carry

        lax.fori_loop(0, n_exp, start_pad, 0)
        lax.fori_loop(0, n_exp, wait_pad, 0)

        def per_block(b, c):
            r0 = pl.multiple_of(b * blk_rows, blk_rows)
            cp = pltpu.make_async_copy(zero_scr, xs_ref.at[pl.ds(r0, blk_rows)], zsem)
            cp.start()
            cp.wait()
            return c

        lax.fori_loop(used_ref[0], n_blocks, per_block, 0)

    def send(j, carry):
        t0 = j * SEND_GROUP
        rows = [dest_ref[(t0 + g) * top_k + k] for g in range(SEND_GROUP) for k in range(top_k)]
        for g in range(SEND_GROUP):
            for k in range(top_k):
                token_row(t0 + g, rows[g * top_k + k]).start(priority=(g * top_k + k) % 2)
        return carry

    lax.fori_loop(0, tr // SEND_GROUP, send, 0)
    for _ in range(top_k):
        pltpu.make_async_copy(h_ref, xs_ref.at[pl.ds(0, tr * c)], sem).wait()


def _dispatch(pad_start, pad_cnt, n_used, dest_flat, h2_tiles, n_blocks, tr, c):
    T = h2_tiles.shape[0] // c
    n_exp = pad_start.shape[0]
    grid_spec = pltpu.PrefetchScalarGridSpec(
        num_scalar_prefetch=3,
        grid=(T // tr,),
        in_specs=[
            pl.BlockSpec((tr * TOP_K,), lambda i, *_: (i,), memory_space=pltpu.SMEM),
            pl.BlockSpec((tr * c, LANES), lambda i, *_: (i, 0)),
        ],
        out_specs=pl.BlockSpec(memory_space=pl.ANY),
        scratch_shapes=[
            pltpu.VMEM((MOE_ROWS * c, LANES), F32),
            pltpu.SemaphoreType.DMA(()),
            pltpu.SemaphoreType.DMA(()),
        ],
    )
    return pl.pallas_call(
        functools.partial(_dispatch_kernel, n_exp=n_exp, n_blocks=n_blocks, top_k=TOP_K, c=c),
        grid_spec=grid_spec,
        out_shape=jax.ShapeDtypeStruct((n_blocks * MOE_ROWS * c, LANES), F32),
        compiler_params=_params("arbitrary"),
        name="dispatch",
    )(pad_start, pad_cnt, n_used, dest_flat, h2_tiles)


def _expert_kernel(owner_ref, used_ref, x_ref, wu_ref, bu_ref, wd_ref, bd_ref, o_ref, wu_scr, wd_scr, *, d_ff, c):
    i = pl.program_id(0)
    rows = x_ref.shape[0] // c
    used = i < used_ref[0]
    new_expert = (i == 0) | (owner_ref[i] != owner_ref[jnp.maximum(i - 1, 0)])

    @pl.when(used & new_expert)
    def _():
        wu_scr[...] = wu_ref[0].astype(BF16)
        wd_scr[...] = wd_ref[0].astype(BF16)

    @pl.when(used)
    def _():
        pr = rows // EXPERT_PARTS
        for part in range(EXPERT_PARTS):
            rs = pl.ds(part * pr * c, pr * c)
            x = _load_token_tiles(x_ref.at[rs], pr, c).astype(BF16)
            gu = jnp.dot(x, wu_scr[...], preferred_element_type=F32) + bu_ref[0]
            g = jnp.minimum(gu[:, :d_ff], SWIGLU_LIMIT)
            u = jnp.clip(gu[:, d_ff:], -SWIGLU_LIMIT, SWIGLU_LIMIT)
            act = (u + 1.0) * (g * _sigmoid(SWIGLU_ALPHA * g))
            y = jnp.dot(act.astype(BF16), wd_scr[...], preferred_element_type=F32) + bd_ref[0]
            _store_token_tiles(o_ref.at[rs], y)

    @pl.when(i >= used_ref[0])
    def _():
        o_ref[...] = jnp.zeros_like(o_ref)


def _experts(owner, n_used, xs_tiles, w_up, b_up, w_down, b_down):
    n_exp, D, two_f = w_up.shape
    c = D // LANES
    d_ff = two_f // 2
    n_blocks = xs_tiles.shape[0] // (MOE_ROWS * c)
    grid_spec = pltpu.PrefetchScalarGridSpec(
        num_scalar_prefetch=2,
        grid=(n_blocks,),
        in_specs=[
            pl.BlockSpec((MOE_ROWS * c, LANES), lambda i, own, used: (jnp.minimum(i, used[0] - 1), 0)),
            pl.BlockSpec((1, D, two_f), lambda i, own, used: (own[i], 0, 0)),
            pl.BlockSpec((1, 1, two_f), lambda i, own, used: (own[i], 0, 0)),
            pl.BlockSpec((1, d_ff, D), lambda i, own, used: (own[i], 0, 0)),
            pl.BlockSpec((1, 1, D), lambda i, own, used: (own[i], 0, 0)),
        ],
        out_specs=pl.BlockSpec((MOE_ROWS * c, LANES), lambda i, own, used: (i, 0)),
        scratch_shapes=[pltpu.VMEM((D, two_f), BF16), pltpu.VMEM((d_ff, D), BF16)],
    )
    return pl.pallas_call(
        functools.partial(_expert_kernel, d_ff=d_ff, c=c),
        grid_spec=grid_spec,
        out_shape=jax.ShapeDtypeStruct(xs_tiles.shape, F32),
        compiler_params=_params("arbitrary", vmem=VMEM_LIMIT_EXPERTS),
        name="experts",
    )(owner, n_used, xs_tiles, w_up, b_up.reshape(n_exp, 1, two_f), w_down, b_down.reshape(n_exp, 1, D))


def _combine_kernel(dest_ref, gate_ref, h2_ref, g_ref, b_ref, y_ref, o_ref, ybuf, sem, *, alpha, top_k, c, parts):
    tr = h2_ref.shape[0] // c
    pr = tr // parts

    def tile(ref, r):
        return ref.at[pl.ds(pl.multiple_of(r * c, c), c)]

    for part in range(parts):
        def start(j, carry, part=part):
            t0 = j * SEND_GROUP
            base = (part * pr + t0) * top_k
            rows = [dest_ref[base + g * top_k + k] for g in range(SEND_GROUP) for k in range(top_k)]
            for g in range(SEND_GROUP):
                for k in range(top_k):
                    pltpu.make_async_copy(tile(y_ref, rows[g * top_k + k]), tile(ybuf.at[part, k], t0 + g),
                                          sem.at[part]).start(priority=(g * top_k + k) % 2)
            return carry

        lax.fori_loop(0, pr // SEND_GROUP, start, 0)

    for part in range(parts):
        for k in range(top_k):
            pltpu.make_async_copy(y_ref.at[pl.ds(0, pr * c)], ybuf.at[part, k], sem.at[part]).wait()
        rs = slice(part * pr, (part + 1) * pr)
        gate = gate_ref[rs, :]
        f = alpha * _load_token_tiles(h2_ref.at[pl.ds(part * pr * c, pr * c)], pr, c)
        for k in range(top_k):
            f = f + gate[:, k:k + 1] * _load_token_tiles(ybuf.at[part, k], pr, c)
        o_ref[rs, :] = _ln_rows(f, g_ref[...], b_ref[...])


def _combine(dest_flat, gate, h2_tiles, ln_g, ln_b, y_tiles, alpha, tr, c):
    T = h2_tiles.shape[0] // c
    D = c * LANES
    return pl.pallas_call(
        functools.partial(_combine_kernel, alpha=alpha, top_k=TOP_K, c=c, parts=COMBINE_PARTS),
        grid=(T // tr,),
        in_specs=[
            pl.BlockSpec((tr * TOP_K,), lambda i: (i,), memory_space=pltpu.SMEM),
            pl.BlockSpec((tr, LANES), lambda i: (i, 0)),
            pl.BlockSpec((tr * c, LANES), lambda i: (i, 0)),
            pl.BlockSpec((1, D), lambda i: (0, 0)),
            pl.BlockSpec((1, D), lambda i: (0, 0)),
            pl.BlockSpec(memory_space=pl.ANY),
        ],
        out_specs=pl.BlockSpec((tr, D), lambda i: (i, 0)),
        out_shape=jax.ShapeDtypeStruct((T, D), F32),
        scratch_shapes=[pltpu.VMEM((COMBINE_PARTS, TOP_K, tr // COMBINE_PARTS * c, LANES), F32),
                        pltpu.SemaphoreType.DMA((COMBINE_PARTS,))],
        compiler_params=_params("arbitrary"),
        name="combine",
    )(dest_flat, gate, h2_tiles, ln_g, ln_b, y_tiles)


def _tile(n, want):
    t = min(n, want)
    assert n % t == 0, (n, want)
    return t


class _Tiles(NamedTuple):
    in_proj_rows: int
    in_proj_cols: int
    mixer_rows: int
    merge_rows: int
    route_rows: int
    move_rows: int


def _plan(T, n_cols):
    cols = n_cols // 4 if n_cols % (4 * 2 * LANES) == 0 else n_cols
    return _Tiles(_tile(T, 512), cols, _tile(T, 512), _tile(T, 512), _tile(T, 512), _tile(T, ROUTE_ROWS))


def kernel(x, ln_in_g, ln_in_b, w_in, gmlp_ln_g, gmlp_ln_b, w_spatial, b_spatial, w_branch_a, w_branch_b,
           w_out, ln_mix_g, ln_mix_b, w_router, b_router, w_up, b_up, w_down, b_down, ln_ffn_g, ln_ffn_b):
    B, S, D = x.shape
    depth = w_in.shape[0]
    assert depth == 1
    T = B * S
    n_exp = w_router.shape[-1]
    dh = D // N_HEADS
    alpha = (2 * depth) ** 0.25
    assert S % MOBA_BLOCK == 0 and T % ROUTE_ROWS == 0 and n_exp <= LANES
    row = lambda a: a.reshape(1, -1)

    x2 = x.reshape(T, D)
    tiles = _plan(T, w_in.shape[-1])
    p = _in_proj(x2, row(ln_in_g), row(ln_in_b), w_in[0].astype(BF16), tiles.in_proj_rows, tiles.in_proj_cols)
    y_a = _mixer_a(p, row(gmlp_ln_g[0]), row(gmlp_ln_b[0]), w_spatial[0], b_spatial[0], D, tiles.mixer_rows)
    y_b = _moba(p, B, S, N_HEADS, dh, 2 * N_HEADS, 3 * N_HEADS, 4 * N_HEADS)

    w_r = jnp.zeros((D, LANES), BF16).at[:, :n_exp].set(w_router[0].astype(BF16))
    b_r = jnp.zeros((1, LANES), F32).at[0, :n_exp].set(b_router[0])
    h2, logits = _merge(x2, row(ln_in_g), row(ln_in_b), y_a, y_b, p, 5, 6,
                        w_branch_a[0].astype(BF16), w_branch_b[0].astype(BF16), w_out[0].astype(BF16),
                        row(ln_mix_g[0]), row(ln_mix_b[0]), w_r, b_r, alpha, tiles.merge_rows)

    dest, gate, meta = _route(logits, n_exp, tiles.route_rows)
    n_blocks = -(-T * TOP_K // MOE_ROWS) + n_exp
    counts, starts, ends = meta[0, :n_exp], meta[1, :n_exp], meta[2, :n_exp]
    owner = meta[SUBLANES:].reshape(-1)[:n_blocks]
    n_used = (ends[n_exp - 1:] // MOE_ROWS).astype(I32)
    dest_flat = dest[:, :TOP_K].reshape(-1)

    c = D // LANES
    xs = _dispatch(starts + counts, ends - starts - counts, n_used, dest_flat, h2, n_blocks, tiles.move_rows, c)
    y = _experts(owner, n_used, xs, w_up[0], b_up[0], w_down[0], b_down[0])
    out = _combine(dest_flat, gate, h2, row(ln_ffn_g[0]), row(ln_ffn_b[0]), y, alpha, tiles.move_rows, c)
    return out.reshape(B, S, D)
```

```python
import functools
import math
from typing import NamedTuple

import jax
import jax.numpy as jnp
from jax import lax
from jax.experimental import pallas as pl
from jax.experimental.pallas import tpu as pltpu

F32 = jnp.float32
BF16 = jnp.bfloat16
I32 = jnp.int32

LN_EPS = 1e-5
LOG2_E = math.log2(math.e)
N_HEADS = 8
MOBA_BLOCK = 256
MOBA_TOPK = 3
MOBA_STREAMS = 4
TOP_K = 4
SWIGLU_LIMIT = 7.0
SWIGLU_ALPHA = 1.702
LANES = 128
SUBLANES = 8
MOE_ROWS = 512
ROUTE_ROWS = 512
SEND_GROUP = 4
COMBINE_PARTS = 4
MERGE_PARTS = 2
EXPERT_PARTS = 2
VMEM_LIMIT = 48 * 1024 * 1024
VMEM_LIMIT_EXPERTS = 56 * 1024 * 1024


def _ln_rows(x, g, b):
    mu = jnp.mean(x, axis=-1, keepdims=True)
    xc = x - mu
    var = jnp.mean(xc * xc, axis=-1, keepdims=True)
    return xc * lax.rsqrt(var + LN_EPS) * g + b


def _gelu(x):
    return x * (lax.erf(x / math.sqrt(2.0)) + 1.0) / 2.0


def _sigmoid(x):
    return 1.0 / (1.0 + jnp.exp(-x))


def _params(*sem, vmem=VMEM_LIMIT):
    return pltpu.CompilerParams(dimension_semantics=sem, vmem_limit_bytes=vmem)


def _store_token_tiles(ref, x):
    rows, d = x.shape
    c = d // LANES
    for s in range(c):
        ref[pl.ds(s, rows, stride=c), :] = x[:, s * LANES:(s + 1) * LANES]


def _load_token_tiles(ref, rows, c):
    return jnp.concatenate([ref[pl.ds(s, rows, stride=c), :] for s in range(c)], axis=1)


def _inproj_kernel(x_ref, g_ref, b_ref, w_ref, o_ref, *, tn, acts):
    h = _ln_rows(x_ref[...], g_ref[...], b_ref[...]).astype(BF16)
    for j, act in enumerate(acts):
        cs = slice(j * tn, (j + 1) * tn)
        r = jnp.dot(h, w_ref[:, cs], preferred_element_type=F32)
        o_ref[:, cs] = (r if act is None else act(r)).astype(BF16)


def _in_proj(x2, ln_g, ln_b, w_in, tm, acts):
    T, D = x2.shape
    N = w_in.shape[1]
    assert N == D * len(acts)
    return pl.pallas_call(
        functools.partial(_inproj_kernel, tn=D, acts=acts),
        grid=(T // tm,),
        in_specs=[
            pl.BlockSpec((tm, D), lambda i: (i, 0)),
            pl.BlockSpec((1, D), lambda i: (0, 0)),
            pl.BlockSpec((1, D), lambda i: (0, 0)),
            pl.BlockSpec((D, N), lambda i: (0, 0)),
        ],
        out_specs=pl.BlockSpec((tm, N), lambda i: (i, 0)),
        out_shape=jax.ShapeDtypeStruct((T, N), BF16),
        compiler_params=_params("parallel"),
        name="in_proj",
    )(x2, ln_g, ln_b, w_in)


def _mixer_a_kernel(u_ref, v_ref, lng_ref, lnb_ref, ws_ref, bs_ref, o_ref, *, chunk, groups):
    tr, width = u_ref.shape
    gd = width // groups
    u = u_ref[...].astype(F32)
    v = v_ref[...].astype(F32)
    vn = _ln_rows(v, lng_ref[...], lnb_ref[...]).astype(BF16)
    row = lax.broadcasted_iota(I32, (chunk, chunk), 0)
    col = lax.broadcasted_iota(I32, (chunk, chunk), 1)
    causal = col <= row
    for g in range(groups):
        w = jnp.where(causal, ws_ref[g], 0.0).astype(BF16)
        for c in range(tr // chunk):
            rs = slice(c * chunk, (c + 1) * chunk)
            cs = slice(g * gd, (g + 1) * gd)
            vs = jnp.dot(w, vn[rs, cs], preferred_element_type=F32) + bs_ref[g]
            o_ref[rs, cs] = (u[rs, cs] * vs).astype(BF16)


def _mixer_a(p, ln_g, ln_b, w_s, b_s, width, tr):
    T = p.shape[0]
    groups, chunk, _ = w_s.shape
    gd = width // groups
    bias = jnp.broadcast_to(b_s[:, :, None], (groups, chunk, gd))
    return pl.pallas_call(
        functools.partial(_mixer_a_kernel, chunk=chunk, groups=groups),
        grid=(T // tr,),
        in_specs=[
            pl.BlockSpec((tr, width), lambda i: (i, 0)),
            pl.BlockSpec((tr, width), lambda i: (i, 1)),
            pl.BlockSpec((1, width), lambda i: (0, 0)),
            pl.BlockSpec((1, width), lambda i: (0, 0)),
            pl.BlockSpec((groups, chunk, chunk), lambda i: (0, 0, 0)),
            pl.BlockSpec((groups, chunk, gd), lambda i: (0, 0, 0)),
        ],
        out_specs=pl.BlockSpec((tr, width), lambda i: (i, 0)),
        out_shape=jax.ShapeDtypeStruct((T, width), BF16),
        compiler_params=_params("parallel"),
        name="mixer_a",
    )(p, p, ln_g, ln_b, w_s, bias)


_NT = (((1,), (1,)), ((), ()))


def _moba_kernel(q_ref, k_ref, v_ref, o_ref, kmean_scr, vt_scr, *stream_scr, blk, topk, scale):
    S, dh = q_ref.shape
    nb = S // blk
    for n in range(nb):
        rs = slice(n * blk, (n + 1) * blk)
        kmean_scr[n:n + 1, :] = jnp.mean(k_ref[rs, :].astype(F32), axis=0, keepdims=True)
        vt_scr[:, rs] = v_ref[rs, :].astype(F32).T.astype(BF16)
    kmean = kmean_scr[...].astype(BF16)

    nidx = lax.broadcasted_iota(I32, (nb, blk), 0)
    kidx = lax.broadcasted_iota(I32, (blk, blk), 0)
    qidx = lax.broadcasted_iota(I32, (blk, blk), 1)
    causal = jnp.where(kidx <= qidx, 0.0, -jnp.inf)

    def fold(x, op):
        return op(x.reshape(blk // SUBLANES, SUBLANES, blk), axis=0)

    def query_block(qb, s_scr, p_scr):
        q = q_ref[qb * blk:(qb + 1) * blk, :]
        keys = (qb + 1) * blk
        if qb > 0:
            gate = lax.dot_general(kmean, q, _NT, preferred_element_type=F32)
            past = nidx < qb
            gate = jnp.where(past, gate, -jnp.inf)
            rank = jnp.zeros((nb, blk), I32)
            for m in range(qb):
                gm = gate[m:m + 1, :]
                rank = rank + jnp.where(gm > gate, 1, jnp.where((gm == gate) & (nidx > m), 1, 0))
            bias = jnp.where(past & (rank < topk), 0.0, -jnp.inf)
        yield

        yield
        mx = None
        for n in range(qb + 1):
            rs = slice(n * blk, (n + 1) * blk)
            s = lax.dot_general(k_ref[rs, :], q, _NT, preferred_element_type=F32)
            s = s + (causal if n == qb else bias[n:n + 1, :])
            s_scr[rs, :] = s
            part = fold(s, jnp.max)
            mx = part if mx is None else jnp.maximum(mx, part)
        m_raw = jnp.max(mx, axis=0, keepdims=True)
        yield

        l8 = None
        for n in range(qb + 1):
            rs = slice(n * blk, (n + 1) * blk)
            p = jnp.exp2((s_scr[rs, :] - m_raw) * (scale * LOG2_E))
            p_scr[rs, :] = p.astype(BF16)
            part = fold(p, jnp.sum)
            l8 = part if l8 is None else l8 + part
        l_fin = jnp.sum(l8, axis=0, keepdims=True)
        yield
        acc = jnp.dot(vt_scr[:, 0:keys], p_scr[0:keys, :], preferred_element_type=F32)
        o_ref[qb * blk:(qb + 1) * blk, :] = (acc * (1.0 / l_fin)).T.astype(BF16)
        yield

    n_streams = len(stream_scr) // 2
    order = _moba_order(nb)
    for g in range(0, nb, n_streams):
        streams = [query_block(qb, stream_scr[2 * i], stream_scr[2 * i + 1])
                   for i, qb in enumerate(order[g:g + n_streams])]
        for _ in range(5):
            for st in streams:
                next(st)


def _moba_order(nb):
    order = []
    for a in range((nb + 1) // 2):
        order += [a] if a == nb - 1 - a else [a, nb - 1 - a]
    return order


def _moba(p, B, S, n_heads, dh, q_col, k_col, v_col):
    T = p.shape[0]
    nb = S // MOBA_BLOCK
    kern = functools.partial(_moba_kernel, blk=MOBA_BLOCK, topk=MOBA_TOPK, scale=dh ** -0.5)
    order = _moba_order(nb)
    n_streams = min(MOBA_STREAMS, nb)
    slot_keys = [max(order[i::n_streams]) * MOBA_BLOCK + MOBA_BLOCK for i in range(n_streams)]
    stream_scratch = []
    for keys in slot_keys:
        stream_scratch += [pltpu.VMEM((keys, MOBA_BLOCK), F32),
                           pltpu.VMEM((keys, MOBA_BLOCK), BF16)]
    return pl.pallas_call(
        kern,
        grid=(B, n_heads),
        in_specs=[
            pl.BlockSpec((S, dh), lambda b, h: (b, q_col + h)),
            pl.BlockSpec((S, dh), lambda b, h: (b, k_col + h)),
            pl.BlockSpec((S, dh), lambda b, h: (b, v_col + h)),
        ],
        out_specs=pl.BlockSpec((S, dh), lambda b, h: (b, h)),
        out_shape=jax.ShapeDtypeStruct((T, n_heads * dh), BF16),
        scratch_shapes=[
            pltpu.VMEM((nb, dh), F32),
            pltpu.VMEM((dh, S), BF16),
        ] + stream_scratch,
        compiler_params=_params("parallel", "parallel"),
        name="moba",
    )(p, p, p)


def _merge_kernel(x_ref, lg_ref, lb_ref, ya_ref, yb_ref, ga_ref, gb_ref, wa_ref, wb_ref, wo_ref,
                  mg_ref, mb_ref, wr_ref, br_ref, h2_ref, logit_ref, cnt_ref, *, alpha, n_exp):
    tm, d = x_ref.shape
    c = d // LANES
    pr = tm // MERGE_PARTS
    chosen = jnp.zeros((1, LANES), F32)
    for part in range(MERGE_PARTS):
        rs = slice(part * pr, (part + 1) * pr)
        h = _ln_rows(x_ref[rs, :], lg_ref[...], lb_ref[...])
        a = jnp.dot(ya_ref[rs, :], wa_ref[...], preferred_element_type=F32)
        b = jnp.dot(yb_ref[rs, :], wb_ref[...], preferred_element_type=F32)
        merged = ga_ref[rs, :].astype(F32) * a + gb_ref[rs, :].astype(F32) * b
        z = jnp.dot(merged.astype(BF16), wo_ref[...], preferred_element_type=F32)
        h2 = _ln_rows(alpha * h + z, mg_ref[...], mb_ref[...])
        _store_token_tiles(h2_ref.at[pl.ds(part * pr * c, pr * c)], h2)
        logits = jnp.dot(h2.astype(BF16), wr_ref[...], preferred_element_type=F32) + br_ref[...]
        logit_ref[rs, :] = logits
        chosen = chosen + _top_experts(logits, n_exp, TOP_K)[3]
    cnt_ref[...] = jnp.broadcast_to(chosen, cnt_ref.shape)


def _merge(x2, ln_g, ln_b, y_a, y_b, p, ga_col, gb_col, w_a, w_b, w_o, mix_g, mix_b, w_r, b_r, alpha, tm, n_exp):
    T, D = x2.shape
    row = lambda i: (i, 0)
    fixed = lambda i: (0, 0)
    return pl.pallas_call(
        functools.partial(_merge_kernel, alpha=alpha, n_exp=n_exp),
        grid=(T // tm,),
        in_specs=[
            pl.BlockSpec((tm, D), row),
            pl.BlockSpec((1, D), fixed),
            pl.BlockSpec((1, D), fixed),
            pl.BlockSpec((tm, D), row),
            pl.BlockSpec((tm, D), row),
            pl.BlockSpec((tm, D), lambda i: (i, ga_col)),
            pl.BlockSpec((tm, D), lambda i: (i, gb_col)),
            pl.BlockSpec((D, D), fixed),
            pl.BlockSpec((D, D), fixed),
            pl.BlockSpec((D, D), fixed),
            pl.BlockSpec((1, D), fixed),
            pl.BlockSpec((1, D), fixed),
            pl.BlockSpec((D, LANES), fixed),
            pl.BlockSpec((1, LANES), fixed),
        ],
        out_specs=[pl.BlockSpec((tm * (D // LANES), LANES), row), pl.BlockSpec((tm, LANES), row),
                   pl.BlockSpec((SUBLANES, LANES), row)],
        out_shape=[jax.ShapeDtypeStruct((T * (D // LANES), LANES), F32),
                   jax.ShapeDtypeStruct((T, LANES), F32),
                   jax.ShapeDtypeStruct((T // tm * SUBLANES, LANES), F32)],
        compiler_params=_params("parallel"),
        name="merge",
    )(x2, ln_g, ln_b, y_a, y_b, p, p, w_a, w_b, w_o, mix_g, mix_b, w_r, b_r)


META_ROWS = 2 * SUBLANES


def _top_experts(logits, n_exp, top_k):
    rows = logits.shape[0]
    lane = lax.broadcasted_iota(I32, (rows, LANES), 1)
    lane_f = lane.astype(F32)
    lg = jnp.where(lane < n_exp, logits, -jnp.inf)
    hots, vals = [], []
    for _ in range(top_k):
        mk = jnp.max(lg, axis=1, keepdims=True)
        ik = jnp.min(jnp.where(lg == mk, lane_f, float(LANES)), axis=1, keepdims=True)
        hot = lane_f == ik
        hots.append(hot)
        vals.append(mk)
        lg = jnp.where(hot, -jnp.inf, lg)
    multi = jnp.zeros((rows, LANES), F32)
    for hot in hots:
        multi = multi + jnp.where(hot, 1.0, 0.0)
    return hots, vals, multi, jnp.sum(multi, axis=0, keepdims=True)


def _route_kernel(logit_ref, tile_cnt_ref, dest_ref, gate_ref, meta_ref, run_scr, start_scr,
                  *, n_exp, top_k, blk_rows):
    i = pl.program_id(0)
    tr = logit_ref.shape[0]
    lane = lax.broadcasted_iota(I32, (tr, LANES), 1)
    hots, vals, multi, colsum = _top_experts(logit_ref[...], n_exp, top_k)

    @pl.when(i == 0)
    def _():
        lane1 = lax.broadcasted_iota(I32, (1, LANES), 1)
        tiles = tile_cnt_ref.shape[0] // SUBLANES
        cnt = jnp.sum(tile_cnt_ref[...].reshape(tiles, SUBLANES, LANES), axis=0)[0:1, :]
        padded = ((cnt.astype(I32) + (blk_rows - 1)) & (-blk_rows)).astype(F32)
        ends = padded
        shift = 1
        while shift < n_exp:
            ends = ends + jnp.where(lane1 >= shift, pltpu.roll(ends, shift, 1), 0.0)
            shift *= 2
        starts = ends - padded
        start_scr[...] = starts
        run_scr[...] = jnp.zeros_like(run_scr)
        sub = lax.broadcasted_iota(I32, (SUBLANES, LANES), 0)
        lane8 = lax.broadcasted_iota(I32, (SUBLANES, LANES), 1)
        first_row = ((sub * LANES + lane8) * blk_rows).astype(F32)
        owner = jnp.zeros((SUBLANES, LANES), I32)
        for e in range(n_exp):
            end_e = jnp.sum(jnp.where(lane1 == e, ends, 0.0), axis=1, keepdims=True)
            owner = owner + jnp.where(end_e <= first_row, 1, 0)
        meta_ref[...] = jnp.zeros_like(meta_ref)
        meta_ref[0:1, :] = cnt.astype(I32)
        meta_ref[1:2, :] = starts.astype(I32)
        meta_ref[2:3, :] = ends.astype(I32)
        meta_ref[SUBLANES:2 * SUBLANES, :] = jnp.minimum(owner, n_exp - 1)

    r = lax.broadcasted_iota(I32, (tr, tr), 0)
    c = lax.broadcasted_iota(I32, (tr, tr), 1)
    before = jnp.where(c < r, 1.0, 0.0).astype(BF16)
    rank = jnp.dot(before, multi.astype(BF16), preferred_element_type=F32)
    pos = rank + (start_scr[...] + run_scr[...])
    denom = jnp.zeros((tr, 1), F32)
    exps = []
    for k in range(top_k):
        ek = jnp.exp(vals[k] - vals[0])
        exps.append(ek)
        denom = denom + ek
    dest = jnp.zeros((tr, LANES), F32)
    gate = jnp.zeros((tr, LANES), F32)
    for k in range(top_k):
        dk = jnp.sum(jnp.where(hots[k], pos, 0.0), axis=1, keepdims=True)
        dest = jnp.where(lane == k, dk, dest)
        gate = jnp.where(lane == k, exps[k] / denom, gate)
    dest_ref[...] = dest.astype(I32)
    gate_ref[...] = gate
    run_scr[...] += colsum


def _route(logits, tile_counts, n_exp, tr):
    T = logits.shape[0]
    tile = lambda i: (i, 0)
    return pl.pallas_call(
        functools.partial(_route_kernel, n_exp=n_exp, top_k=TOP_K, blk_rows=MOE_ROWS),
        grid=(T // tr,),
        in_specs=[pl.BlockSpec((tr, LANES), tile), pl.BlockSpec(tile_counts.shape, lambda i: (0, 0))],
        out_specs=[
            pl.BlockSpec((tr, LANES), tile),
            pl.BlockSpec((tr, LANES), tile),
            pl.BlockSpec((META_ROWS, LANES), lambda i: (0, 0)),
        ],
        out_shape=[
            jax.ShapeDtypeStruct((T, LANES), I32),
            jax.ShapeDtypeStruct((T, LANES), F32),
            jax.ShapeDtypeStruct((META_ROWS, LANES), I32),
        ],
        scratch_shapes=[pltpu.VMEM((1, LANES), F32)] * 2,
        compiler_params=_params("arbitrary"),
        name="route",
    )(logits, tile_counts)


def _dispatch_kernel(pad_start_ref, pad_cnt_ref, used_ref, dest_ref, h_ref, xs_ref, zero_scr, sem, zsem,
                     *, n_exp, n_blocks, top_k, c):
    i = pl.program_id(0)
    tr = h_ref.shape[0] // c
    blk_rows = zero_scr.shape[0]

    def tile(ref, r):
        return ref.at[pl.ds(pl.multiple_of(r * c, c), c)]

    def zero_rows(dst_row, n_rows):
        dst = xs_ref.at[pl.ds(pl.multiple_of(dst_row * c, c), n_rows * c)]
        return pltpu.make_async_copy(zero_scr.at[pl.ds(0, n_rows * c)], dst, zsem)

    def zero_pad(e, wait):
        at = pad_start_ref[e]
        left = pad_cnt_ref[e]
        size = blk_rows // c // 2
        while size >= 1:
            has = (left & size) != 0

            @pl.when(has)
            def _(at=at, size=size):
                cp = zero_rows(at, size)
                cp.wait() if wait else cp.start()

            at = at + jnp.where(has, size, 0)
            size //= 2

    def token_row(t, dst_row):
        return pltpu.make_async_copy(tile(h_ref, t), tile(xs_ref, dst_row), sem)

    @pl.when(i == 0)
    def _():
        zero_scr[...] = jnp.zeros_like(zero_scr)

        def start_pad(e, carry):
            zero_pad(e, wait=False)
            return carry

        def wait_pad(e, carry):
            zero_pad(e, wait=True)
            return carry

        lax.fori_loop(0, n_exp, start_pad, 0)
        lax.fori_loop(0, n_exp, wait_pad, 0)

        def per_block(b, c):
            r0 = pl.multiple_of(b * blk_rows, blk_rows)
            cp = pltpu.make_async_copy(zero_scr, xs_ref.at[pl.ds(r0, blk_rows)], zsem)
            cp.start()
            cp.wait()
            return c

        lax.fori_loop(used_ref[0], n_blocks, per_block, 0)

    def send(j, carry):
        t0 = j * SEND_GROUP
        rows = [dest_ref[(t0 + g) * top_k + k] for g in range(SEND_GROUP) for k in range(top_k)]
        for g in range(SEND_GROUP):
            for k in range(top_k):
                token_row(t0 + g, rows[g * top_k + k]).start(priority=(g * top_k + k) % 2)
        return carry

    lax.fori_loop(0, tr // SEND_GROUP, send, 0)
    for _ in range(top_k):
        pltpu.make_async_copy(h_ref, xs_ref.at[pl.ds(0, tr * c)], sem).wait()


def _dispatch(pad_start, pad_cnt, n_used, dest_flat, h2_tiles, n_blocks, tr, c):
    T = h2_tiles.shape[0] // c
    n_exp = pad_start.shape[0]
    grid_spec = pltpu.PrefetchScalarGridSpec(
        num_scalar_prefetch=3,
        grid=(T // tr,),
        in_specs=[
            pl.BlockSpec((tr * TOP_K,), lambda i, *_: (i,), memory_space=pltpu.SMEM),
            pl.BlockSpec((tr * c, LANES), lambda i, *_: (i, 0)),
        ],
        out_specs=pl.BlockSpec(memory_space=pl.ANY),
        scratch_shapes=[
            pltpu.VMEM((MOE_ROWS * c, LANES), F32),
            pltpu.SemaphoreType.DMA(()),
            pltpu.SemaphoreType.DMA(()),
        ],
    )
    return pl.pallas_call(
        functools.partial(_dispatch_kernel, n_exp=n_exp, n_blocks=n_blocks, top_k=TOP_K, c=c),
        grid_spec=grid_spec,
        out_shape=jax.ShapeDtypeStruct((n_blocks * MOE_ROWS * c, LANES), F32),
        compiler_params=_params("arbitrary"),
        name="dispatch",
    )(pad_start, pad_cnt, n_used, dest_flat, h2_tiles)


def _expert_kernel(owner_ref, used_ref, x_ref, wu_ref, bu_ref, wd_ref, bd_ref, o_ref, wu_scr, wd_scr, *, d_ff, c):
    i = pl.program_id(0)
    rows = x_ref.shape[0] // c
    used = i < used_ref[0]
    new_expert = (i == 0) | (owner_ref[i] != owner_ref[jnp.maximum(i - 1, 0)])

    @pl.when(used & new_expert)
    def _():
        wu_scr[...] = wu_ref[0].astype(BF16)
        wd_scr[...] = wd_ref[0].astype(BF16)

    @pl.when(used)
    def _():
        pr = rows // EXPERT_PARTS
        for part in range(EXPERT_PARTS):
            rs = pl.ds(part * pr * c, pr * c)
            x = _load_token_tiles(x_ref.at[rs], pr, c).astype(BF16)
            gu = jnp.dot(x, wu_scr[...], preferred_element_type=F32) + bu_ref[0]
            g = jnp.minimum(gu[:, :d_ff], SWIGLU_LIMIT)
            u = jnp.clip(gu[:, d_ff:], -SWIGLU_LIMIT, SWIGLU_LIMIT)
            act = (u + 1.0) * (g * _sigmoid(SWIGLU_ALPHA * g))
            y = jnp.dot(act.astype(BF16), wd_scr[...], preferred_element_type=F32) + bd_ref[0]
            _store_token_tiles(o_ref.at[rs], y)

    @pl.when(i >= used_ref[0])
    def _():
        o_ref[...] = jnp.zeros_like(o_ref)


def _experts(owner, n_used, xs_tiles, w_up, b_up, w_down, b_down):
    n_exp, D, two_f = w_up.shape
    c = D // LANES
    d_ff = two_f // 2
    n_blocks = xs_tiles.shape[0] // (MOE_ROWS * c)
    grid_spec = pltpu.PrefetchScalarGridSpec(
        num_scalar_prefetch=2,
        grid=(n_blocks,),
        in_specs=[
            pl.BlockSpec((MOE_ROWS * c, LANES), lambda i, own, used: (jnp.minimum(i, used[0] - 1), 0)),
            pl.BlockSpec((1, D, two_f), lambda i, own, used: (own[i], 0, 0)),
            pl.BlockSpec((1, 1, two_f), lambda i, own, used: (own[i], 0, 0)),
            pl.BlockSpec((1, d_ff, D), lambda i, own, used: (own[i], 0, 0)),
            pl.BlockSpec((1, 1, D), lambda i, own, used: (own[i], 0, 0)),
        ],
        out_specs=pl.BlockSpec((MOE_ROWS * c, LANES), lambda i, own, used: (i, 0)),
        scratch_shapes=[pltpu.VMEM((D, two_f), BF16), pltpu.VMEM((d_ff, D), BF16)],
    )
    return pl.pallas_call(
        functools.partial(_expert_kernel, d_ff=d_ff, c=c),
        grid_spec=grid_spec,
        out_shape=jax.ShapeDtypeStruct(xs_tiles.shape, F32),
        compiler_params=_params("arbitrary", vmem=VMEM_LIMIT_EXPERTS),
        name="experts",
    )(owner, n_used, xs_tiles, w_up, b_up.reshape(n_exp, 1, two_f), w_down, b_down.reshape(n_exp, 1, D))


def _combine_kernel(dest_ref, gate_ref, h2_ref, g_ref, b_ref, y_ref, o_ref, ybuf, sem, *, alpha, top_k, c, parts):
    tr = h2_ref.shape[0] // c
    pr = tr // parts

    def tile(ref, r):
        return ref.at[pl.ds(pl.multiple_of(r * c, c), c)]

    for part in range(parts):
        def start(j, carry, part=part):
            t0 = j * SEND_GROUP
            base = (part * pr + t0) * top_k
            rows = [dest_ref[base + g * top_k + k] for g in range(SEND_GROUP) for k in range(top_k)]
            for g in range(SEND_GROUP):
                for k in range(top_k):
                    pltpu.make_async_copy(tile(y_ref, rows[g * top_k + k]), tile(ybuf.at[part, k], t0 + g),
                                          sem.at[part]).start(priority=(g * top_k + k) % 2)
            return carry

        lax.fori_loop(0, pr // SEND_GROUP, start, 0)

    for part in range(parts):
        for k in range(top_k):
            pltpu.make_async_copy(y_ref.at[pl.ds(0, pr * c)], ybuf.at[part, k], sem.at[part]).wait()
        rs = slice(part * pr, (part + 1) * pr)
        gate = gate_ref[rs, :]
        f = alpha * _load_token_tiles(h2_ref.at[pl.ds(part * pr * c, pr * c)], pr, c)
        for k in range(top_k):
            f = f + gate[:, k:k + 1] * _load_token_tiles(ybuf.at[part, k], pr, c)
        o_ref[rs, :] = _ln_rows(f, g_ref[...], b_ref[...])


def _combine(dest_flat, gate, h2_tiles, ln_g, ln_b, y_tiles, alpha, tr, c):
    T = h2_tiles.shape[0] // c
    D = c * LANES
    return pl.pallas_call(
        functools.partial(_combine_kernel, alpha=alpha, top_k=TOP_K, c=c, parts=COMBINE_PARTS),
        grid=(T // tr,),
        in_specs=[
            pl.BlockSpec((tr * TOP_K,), lambda i: (i,), memory_space=pltpu.SMEM),
            pl.BlockSpec((tr, LANES), lambda i: (i, 0)),
            pl.BlockSpec((tr * c, LANES), lambda i: (i, 0)),
            pl.BlockSpec((1, D), lambda i: (0, 0)),
            pl.BlockSpec((1, D), lambda i: (0, 0)),
            pl.BlockSpec(memory_space=pl.ANY),
        ],
        out_specs=pl.BlockSpec((tr, D), lambda i: (i, 0)),
        out_shape=jax.ShapeDtypeStruct((T, D), F32),
        scratch_shapes=[pltpu.VMEM((COMBINE_PARTS, TOP_K, tr // COMBINE_PARTS * c, LANES), F32),
                        pltpu.SemaphoreType.DMA((COMBINE_PARTS,))],
        compiler_params=_params("arbitrary"),
        name="combine",
    )(dest_flat, gate, h2_tiles, ln_g, ln_b, y_tiles)


def _tile(n, want):
    t = min(n, want)
    assert n % t == 0, (n, want)
    return t


class _Tiles(NamedTuple):
    in_proj_rows: int
    mixer_rows: int
    merge_rows: int
    route_rows: int
    move_rows: int


def _plan(T):
    return _Tiles(_tile(T, 512), _tile(T, 512), _tile(T, 512), _tile(T, 512), _tile(T, ROUTE_ROWS))


def kernel(x, ln_in_g, ln_in_b, w_in, gmlp_ln_g, gmlp_ln_b, w_spatial, b_spatial, w_branch_a, w_branch_b,
           w_out, ln_mix_g, ln_mix_b, w_router, b_router, w_up, b_up, w_down, b_down, ln_ffn_g, ln_ffn_b):
    B, S, D = x.shape
    depth = w_in.shape[0]
    assert depth == 1
    T = B * S
    n_exp = w_router.shape[-1]
    dh = D // N_HEADS
    alpha = (2 * depth) ** 0.25
    assert S % MOBA_BLOCK == 0 and T % ROUTE_ROWS == 0 and n_exp <= LANES
    row = lambda a: a.reshape(1, -1)

    x2 = x.reshape(T, D)
    tiles = _plan(T)
    acts = (_gelu, _gelu, None, None, None, _sigmoid, _sigmoid)
    p = _in_proj(x2, row(ln_in_g), row(ln_in_b), w_in[0].astype(BF16), tiles.in_proj_rows, acts)
    y_a = _mixer_a(p, row(gmlp_ln_g[0]), row(gmlp_ln_b[0]), w_spatial[0], b_spatial[0], D, tiles.mixer_rows)
    y_b = _moba(p, B, S, N_HEADS, dh, 2 * N_HEADS, 3 * N_HEADS, 4 * N_HEADS)

    w_r = jnp.zeros((D, LANES), BF16).at[:, :n_exp].set(w_router[0].astype(BF16))
    b_r = jnp.zeros((1, LANES), F32).at[0, :n_exp].set(b_router[0])
    h2, logits, tile_counts = _merge(x2, row(ln_in_g), row(ln_in_b), y_a, y_b, p, 5, 6,
                                     w_branch_a[0].astype(BF16), w_branch_b[0].astype(BF16), w_out[0].astype(BF16),
                                     row(ln_mix_g[0]), row(ln_mix_b[0]), w_r, b_r, alpha, tiles.merge_rows, n_exp)

    dest, gate, meta = _route(logits, tile_counts, n_exp, tiles.route_rows)
    n_blocks = -(-T * TOP_K // MOE_ROWS) + n_exp
    counts, starts, ends = meta[0, :n_exp], meta[1, :n_exp], meta[2, :n_exp]
    owner = meta[SUBLANES:].reshape(-1)[:n_blocks]
    n_used = (ends[n_exp - 1:] // MOE_ROWS).astype(I32)
    dest_flat = dest[:, :TOP_K].reshape(-1)

    c = D // LANES
    xs = _dispatch(starts + counts, ends - starts - counts, n_used, dest_flat, h2, n_blocks, tiles.move_rows, c)
    y = _experts(owner, n_used, xs, w_up[0], b_up[0], w_down[0], b_down[0])
    out = _combine(dest_flat, gate, h2, row(ln_ffn_g[0]), row(ln_ffn_b[0]), y, alpha, tiles.move_rows, c)
    return out.reshape(B, S, D)
```

```python
import functools
import math
from typing import NamedTuple

import jax
import jax.numpy as jnp
from jax import lax
from jax.experimental import pallas as pl
from jax.experimental.pallas import tpu as pltpu

F32 = jnp.float32
BF16 = jnp.bfloat16
I32 = jnp.int32

LN_EPS = 1e-5
LOG2_E = math.log2(math.e)
N_HEADS = 8
MOBA_BLOCK = 256
MOBA_TOPK = 3
MOBA_STREAMS = 4
TOP_K = 4
SWIGLU_LIMIT = 7.0
SWIGLU_ALPHA = 1.702
LANES = 128
SUBLANES = 8
MOE_ROWS = 512
ROUTE_ROWS = 1024
SEND_GROUP = 4
COMBINE_PARTS = 4
MERGE_PARTS = 2
EXPERT_PARTS = 2
VMEM_LIMIT = 48 * 1024 * 1024
VMEM_LIMIT_EXPERTS = 56 * 1024 * 1024


def _ln_rows(x, g, b):
    mu = jnp.mean(x, axis=-1, keepdims=True)
    xc = x - mu
    var = jnp.mean(xc * xc, axis=-1, keepdims=True)
    return xc * lax.rsqrt(var + LN_EPS) * g + b


def _gelu(x):
    return x * (lax.erf(x / math.sqrt(2.0)) + 1.0) / 2.0


def _sigmoid(x):
    return 1.0 / (1.0 + jnp.exp(-x))


def _params(*sem, vmem=VMEM_LIMIT):
    return pltpu.CompilerParams(dimension_semantics=sem, vmem_limit_bytes=vmem)


def _store_token_tiles(ref, x):
    rows, d = x.shape
    c = d // LANES
    for s in range(c):
        ref[pl.ds(s, rows, stride=c), :] = x[:, s * LANES:(s + 1) * LANES]


def _load_token_tiles(ref, rows, c):
    return jnp.concatenate([ref[pl.ds(s, rows, stride=c), :] for s in range(c)], axis=1)


def _inproj_kernel(x_ref, g_ref, b_ref, w_ref, o_ref, *, tn):
    h = _ln_rows(x_ref[...], g_ref[...], b_ref[...]).astype(BF16)
    for j in range(w_ref.shape[1] // tn):
        cs = slice(j * tn, (j + 1) * tn)
        o_ref[:, cs] = jnp.dot(h, w_ref[:, cs], preferred_element_type=F32).astype(BF16)


def _in_proj(x2, ln_g, ln_b, w_in, tm, tn):
    T, D = x2.shape
    N = w_in.shape[1]
    assert N % tn == 0
    return pl.pallas_call(
        functools.partial(_inproj_kernel, tn=tn),
        grid=(T // tm,),
        in_specs=[
            pl.BlockSpec((tm, D), lambda i: (i, 0)),
            pl.BlockSpec((1, D), lambda i: (0, 0)),
            pl.BlockSpec((1, D), lambda i: (0, 0)),
            pl.BlockSpec((D, N), lambda i: (0, 0)),
        ],
        out_specs=pl.BlockSpec((tm, N), lambda i: (i, 0)),
        out_shape=jax.ShapeDtypeStruct((T, N), BF16),
        compiler_params=_params("parallel"),
        name="in_proj",
    )(x2, ln_g, ln_b, w_in)


def _mixer_a_kernel(u_ref, v_ref, lng_ref, lnb_ref, ws_ref, bs_ref, o_ref, *, chunk, groups):
    tr, width = u_ref.shape
    gd = width // groups
    u = _gelu(u_ref[...].astype(F32))
    v = _gelu(v_ref[...].astype(F32))
    vn = _ln_rows(v, lng_ref[...], lnb_ref[...]).astype(BF16)
    row = lax.broadcasted_iota(I32, (chunk, chunk), 0)
    col = lax.broadcasted_iota(I32, (chunk, chunk), 1)
    causal = col <= row
    for g in range(groups):
        w = jnp.where(causal, ws_ref[g], 0.0).astype(BF16)
        for c in range(tr // chunk):
            rs = slice(c * chunk, (c + 1) * chunk)
            cs = slice(g * gd, (g + 1) * gd)
            vs = jnp.dot(w, vn[rs, cs], preferred_element_type=F32) + bs_ref[g]
            o_ref[rs, cs] = (u[rs, cs] * vs).astype(BF16)


def _mixer_a(p, ln_g, ln_b, w_s, b_s, width, tr):
    T = p.shape[0]
    groups, chunk, _ = w_s.shape
    gd = width // groups
    bias = jnp.broadcast_to(b_s[:, :, None], (groups, chunk, gd))
    return pl.pallas_call(
        functools.partial(_mixer_a_kernel, chunk=chunk, groups=groups),
        grid=(T // tr,),
        in_specs=[
            pl.BlockSpec((tr, width), lambda i: (i, 0)),
            pl.BlockSpec((tr, width), lambda i: (i, 1)),
            pl.BlockSpec((1, width), lambda i: (0, 0)),
            pl.BlockSpec((1, width), lambda i: (0, 0)),
            pl.BlockSpec((groups, chunk, chunk), lambda i: (0, 0, 0)),
            pl.BlockSpec((groups, chunk, gd), lambda i: (0, 0, 0)),
        ],
        out_specs=pl.BlockSpec((tr, width), lambda i: (i, 0)),
        out_shape=jax.ShapeDtypeStruct((T, width), BF16),
        compiler_params=_params("parallel"),
        name="mixer_a",
    )(p, p, ln_g, ln_b, w_s, bias)


_NT = (((1,), (1,)), ((), ()))


def _moba_kernel(q_ref, k_ref, v_ref, o_ref, kmean_scr, vt_scr, *stream_scr, blk, topk, scale):
    S, dh = q_ref.shape
    nb = S // blk
    for n in range(nb):
        rs = slice(n * blk, (n + 1) * blk)
        kmean_scr[n:n + 1, :] = jnp.mean(k_ref[rs, :].astype(F32), axis=0, keepdims=True)
        vt_scr[:, rs] = v_ref[rs, :].astype(F32).T.astype(BF16)
    kmean = kmean_scr[...].astype(BF16)

    nidx = lax.broadcasted_iota(I32, (nb, blk), 0)
    kidx = lax.broadcasted_iota(I32, (blk, blk), 0)
    qidx = lax.broadcasted_iota(I32, (blk, blk), 1)
    causal = jnp.where(kidx <= qidx, 0.0, -jnp.inf)

    def fold(x, op):
        return op(x.reshape(blk // SUBLANES, SUBLANES, blk), axis=0)

    def query_block(qb, s_scr, p_scr):
        q = q_ref[qb * blk:(qb + 1) * blk, :]
        keys = (qb + 1) * blk
        if qb > 0:
            gate = lax.dot_general(kmean, q, _NT, preferred_element_type=F32)
            past = nidx < qb
            gate = jnp.where(past, gate, -jnp.inf)
            rank = jnp.zeros((nb, blk), I32)
            for m in range(qb):
                gm = gate[m:m + 1, :]
                rank = rank + jnp.where(gm > gate, 1, jnp.where((gm == gate) & (nidx > m), 1, 0))
            bias = jnp.where(past & (rank < topk), 0.0, -jnp.inf)
        yield
        yield
        mx = None
        for n in range(qb + 1):
            rs = slice(n * blk, (n + 1) * blk)
            s = lax.dot_general(k_ref[rs, :], q, _NT, preferred_element_type=F32)
            s = s + (causal if n == qb else bias[n:n + 1, :])
            s_scr[rs, :] = s
            part = fold(s, jnp.max)
            mx = part if mx is None else jnp.maximum(mx, part)
        m_raw = jnp.max(mx, axis=0, keepdims=True)
        yield

        l8 = None
        for n in range(qb + 1):
            rs = slice(n * blk, (n + 1) * blk)
            p = jnp.exp2((s_scr[rs, :] - m_raw) * (scale * LOG2_E))
            p_scr[rs, :] = p.astype(BF16)
            part = fold(p, jnp.sum)
            l8 = part if l8 is None else l8 + part
        l_fin = jnp.sum(l8, axis=0, keepdims=True)
        yield
        acc = jnp.dot(vt_scr[:, 0:keys], p_scr[0:keys, :], preferred_element_type=F32)
        o_ref[qb * blk:(qb + 1) * blk, :] = (acc * (1.0 / l_fin)).T.astype(BF16)
        yield

    n_streams = len(stream_scr) // 2
    order = _moba_order(nb)
    for g in range(0, nb, n_streams):
        streams = [query_block(qb, stream_scr[2 * i], stream_scr[2 * i + 1])
                   for i, qb in enumerate(order[g:g + n_streams])]
        for _ in range(5):
            for st in streams:
                next(st)


def _moba_order(nb):
    order = []
    for a in range((nb + 1) // 2):
        order += [a] if a == nb - 1 - a else [a, nb - 1 - a]
    return order


def _moba(p, B, S, n_heads, dh, q_col, k_col, v_col):
    T = p.shape[0]
    nb = S // MOBA_BLOCK
    kern = functools.partial(_moba_kernel, blk=MOBA_BLOCK, topk=MOBA_TOPK, scale=dh ** -0.5)
    order = _moba_order(nb)
    n_streams = min(MOBA_STREAMS, nb)
    slot_keys = [max(order[i::n_streams]) * MOBA_BLOCK + MOBA_BLOCK for i in range(n_streams)]
    stream_scratch = []
    for keys in slot_keys:
        stream_scratch += [pltpu.VMEM((keys, MOBA_BLOCK), F32),
                           pltpu.VMEM((keys, MOBA_BLOCK), BF16)]
    return pl.pallas_call(
        kern,
        grid=(B, n_heads),
        in_specs=[
            pl.BlockSpec((S, dh), lambda b, h: (b, q_col + h)),
            pl.BlockSpec((S, dh), lambda b, h: (b, k_col + h)),
            pl.BlockSpec((S, dh), lambda b, h: (b, v_col + h)),
        ],
        out_specs=pl.BlockSpec((S, dh), lambda b, h: (b, h)),
        out_shape=jax.ShapeDtypeStruct((T, n_heads * dh), BF16),
        scratch_shapes=[
            pltpu.VMEM((nb, dh), F32),
            pltpu.VMEM((dh, S), BF16),
        ] + stream_scratch,
        compiler_params=_params("parallel", "parallel"),
        name="moba",
    )(p, p, p)


def _merge_kernel(x_ref, lg_ref, lb_ref, ya_ref, yb_ref, ga_ref, gb_ref, wa_ref, wb_ref, wo_ref,
                  mg_ref, mb_ref, wr_ref, br_ref, h2_ref, logit_ref, *, alpha):
    tm, d = x_ref.shape
    c = d // LANES
    pr = tm // MERGE_PARTS
    for part in range(MERGE_PARTS):
        rs = slice(part * pr, (part + 1) * pr)
        h = _ln_rows(x_ref[rs, :], lg_ref[...], lb_ref[...])
        a = jnp.dot(ya_ref[rs, :], wa_ref[...], preferred_element_type=F32)
        b = jnp.dot(yb_ref[rs, :], wb_ref[...], preferred_element_type=F32)
        merged = _sigmoid(ga_ref[rs, :].astype(F32)) * a + _sigmoid(gb_ref[rs, :].astype(F32)) * b
        z = jnp.dot(merged.astype(BF16), wo_ref[...], preferred_element_type=F32)
        h2 = _ln_rows(alpha * h + z, mg_ref[...], mb_ref[...])
        _store_token_tiles(h2_ref.at[pl.ds(part * pr * c, pr * c)], h2)
        logit_ref[rs, :] = jnp.dot(h2.astype(BF16), wr_ref[...], preferred_element_type=F32) + br_ref[...]


def _merge(x2, ln_g, ln_b, y_a, y_b, p, ga_col, gb_col, w_a, w_b, w_o, mix_g, mix_b, w_r, b_r, alpha, tm):
    T, D = x2.shape
    row = lambda i: (i, 0)
    fixed = lambda i: (0, 0)
    return pl.pallas_call(
        functools.partial(_merge_kernel, alpha=alpha),
        grid=(T // tm,),
        in_specs=[
            pl.BlockSpec((tm, D), row),
            pl.BlockSpec((1, D), fixed),
            pl.BlockSpec((1, D), fixed),
            pl.BlockSpec((tm, D), row),
            pl.BlockSpec((tm, D), row),
            pl.BlockSpec((tm, D), lambda i: (i, ga_col)),
            pl.BlockSpec((tm, D), lambda i: (i, gb_col)),
            pl.BlockSpec((D, D), fixed),
            pl.BlockSpec((D, D), fixed),
            pl.BlockSpec((D, D), fixed),
            pl.BlockSpec((1, D), fixed),
            pl.BlockSpec((1, D), fixed),
            pl.BlockSpec((D, LANES), fixed),
            pl.BlockSpec((1, LANES), fixed),
        ],
        out_specs=[pl.BlockSpec((tm * (D // LANES), LANES), row), pl.BlockSpec((tm, LANES), row)],
        out_shape=[jax.ShapeDtypeStruct((T * (D // LANES), LANES), F32),
                   jax.ShapeDtypeStruct((T, LANES), F32)],
        compiler_params=_params("parallel"),
        name="merge",
    )(x2, ln_g, ln_b, y_a, y_b, p, p, w_a, w_b, w_o, mix_g, mix_b, w_r, b_r)


META_ROWS = 2 * SUBLANES


def _route_kernel(logit_ref, dest_ref, gate_ref, meta_ref, cnt_scr, run_scr, start_scr,
                  *, n_exp, top_k, blk_rows):
    phase = pl.program_id(0)
    i = pl.program_id(1)
    tr = logit_ref.shape[0]
    lane = lax.broadcasted_iota(I32, (tr, LANES), 1)
    lane_f = lane.astype(F32)
    lg = jnp.where(lane < n_exp, logit_ref[...], -jnp.inf)
    hots, vals = [], []
    for _ in range(top_k):
        mk = jnp.max(lg, axis=1, keepdims=True)
        ik = jnp.min(jnp.where(lg == mk, lane_f, float(LANES)), axis=1, keepdims=True)
        hot = lane_f == ik
        hots.append(hot)
        vals.append(mk)
        lg = jnp.where(hot, -jnp.inf, lg)
    multi = jnp.zeros((tr, LANES), F32)
    for hot in hots:
        multi = multi + jnp.where(hot, 1.0, 0.0)
    colsum = jnp.sum(multi, axis=0, keepdims=True)

    @pl.when((phase == 0) & (i == 0))
    def _():
        cnt_scr[...] = jnp.zeros_like(cnt_scr)

    @pl.when(phase == 0)
    def _():
        cnt_scr[...] += colsum

    @pl.when((phase == 1) & (i == 0))
    def _():
        lane1 = lax.broadcasted_iota(I32, (1, LANES), 1)
        cnt = cnt_scr[...]
        padded = ((cnt.astype(I32) + (blk_rows - 1)) & (-blk_rows)).astype(F32)
        ends = padded
        shift = 1
        while shift < n_exp:
            ends = ends + jnp.where(lane1 >= shift, pltpu.roll(ends, shift, 1), 0.0)
            shift *= 2
        starts = ends - padded
        start_scr[...] = starts
        run_scr[...] = jnp.zeros_like(run_scr)
        sub = lax.broadcasted_iota(I32, (SUBLANES, LANES), 0)
        lane8 = lax.broadcasted_iota(I32, (SUBLANES, LANES), 1)
        first_row = ((sub * LANES + lane8) * blk_rows).astype(F32)
        owner = jnp.zeros((SUBLANES, LANES), I32)
        for e in range(n_exp):
            end_e = jnp.sum(jnp.where(lane1 == e, ends, 0.0), axis=1, keepdims=True)
            owner = owner + jnp.where(end_e <= first_row, 1, 0)
        meta_ref[...] = jnp.zeros_like(meta_ref)
        meta_ref[0:1, :] = cnt.astype(I32)
        meta_ref[1:2, :] = starts.astype(I32)
        meta_ref[2:3, :] = ends.astype(I32)
        meta_ref[SUBLANES:2 * SUBLANES, :] = jnp.minimum(owner, n_exp - 1)

    @pl.when(phase == 1)
    def _():
        r = lax.broadcasted_iota(I32, (tr, tr), 0)
        c = lax.broadcasted_iota(I32, (tr, tr), 1)
        before = jnp.where(c < r, 1.0, 0.0).astype(BF16)
        rank = jnp.dot(before, multi.astype(BF16), preferred_element_type=F32)
        pos = rank + (start_scr[...] + run_scr[...])
        denom = jnp.zeros((tr, 1), F32)
        exps = []
        for k in range(top_k):
            ek = jnp.exp(vals[k] - vals[0])
            exps.append(ek)
            denom = denom + ek
        dest = jnp.zeros((tr, LANES), F32)
        gate = jnp.zeros((tr, LANES), F32)
        for k in range(top_k):
            dk = jnp.sum(jnp.where(hots[k], pos, 0.0), axis=1, keepdims=True)
            dest = jnp.where(lane == k, dk, dest)
            gate = jnp.where(lane == k, exps[k] / denom, gate)
        dest_ref[...] = dest.astype(I32)
        gate_ref[...] = gate
        run_scr[...] += colsum


def _route(logits, n_exp, tr):
    T = logits.shape[0]
    tile = lambda ph, i: (i * ph, 0)
    return pl.pallas_call(
        functools.partial(_route_kernel, n_exp=n_exp, top_k=TOP_K, blk_rows=MOE_ROWS),
        grid=(2, T // tr),
        in_specs=[pl.BlockSpec((tr, LANES), lambda ph, i: (i, 0))],
        out_specs=[
            pl.BlockSpec((tr, LANES), tile),
            pl.BlockSpec((tr, LANES), tile),
            pl.BlockSpec((META_ROWS, LANES), lambda ph, i: (0, 0)),
        ],
        out_shape=[
            jax.ShapeDtypeStruct((T, LANES), I32),
            jax.ShapeDtypeStruct((T, LANES), F32),
            jax.ShapeDtypeStruct((META_ROWS, LANES), I32),
        ],
        scratch_shapes=[pltpu.VMEM((1, LANES), F32)] * 3,
        compiler_params=_params("arbitrary", "arbitrary"),
        name="route",
    )(logits)


def _dispatch_kernel(pad_start_ref, pad_cnt_ref, used_ref, dest_ref, h_ref, xs_ref, zero_scr, sem, zsem,
                     *, n_exp, n_blocks, top_k, c):
    i = pl.program_id(0)
    tr = h_ref.shape[0] // c
    blk_rows = zero_scr.shape[0]

    def tile(ref, r):
        return ref.at[pl.ds(pl.multiple_of(r * c, c), c)]

    def zero_rows(dst_row, n_rows):
        dst = xs_ref.at[pl.ds(pl.multiple_of(dst_row * c, c), n_rows * c)]
        return pltpu.make_async_copy(zero_scr.at[pl.ds(0, n_rows * c)], dst, zsem)

    def zero_pad(e, wait):
        at = pad_start_ref[e]
        left = pad_cnt_ref[e]
        size = blk_rows // c // 2
        while size >= 1:
            has = (left & size) != 0

            @pl.when(has)
            def _(at=at, size=size):
                cp = zero_rows(at, size)
                cp.wait() if wait else cp.start()

            at = at + jnp.where(has, size, 0)
            size //= 2

    def token_row(t, dst_row):
        return pltpu.make_async_copy(tile(h_ref, t), tile(xs_ref, dst_row), sem)

    @pl.when(i == 0)
    def _():
        zero_scr[...] = jnp.zeros_like(zero_scr)

        def start_pad(e, carry):
            zero_pad(e, wait=False)
            return carry

        def wait_pad(e, carry):
            zero_pad(e, wait=True)
            return carry

        lax.fori_loop(0, n_exp, start_pad, 0)
        lax.fori_loop(0, n_exp, wait_pad, 0)

        def per_block(b, c):
            r0 = pl.multiple_of(b * blk_rows, blk_rows)
            cp = pltpu.make_async_copy(zero_scr, xs_ref.at[pl.ds(r0, blk_rows)], zsem)
            cp.start()
            cp.wait()
            return c

        lax.fori_loop(used_ref[0], n_blocks, per_block, 0)

    def send(j, carry):
        t0 = j * SEND_GROUP
        rows = [dest_ref[(t0 + g) * top_k + k] for g in range(SEND_GROUP) for k in range(top_k)]
        for g in range(SEND_GROUP):
            for k in range(top_k):
                token_row(t0 + g, rows[g * top_k + k]).start(priority=(g * top_k + k) % 2)
        return carry

    lax.fori_loop(0, tr // SEND_GROUP, send, 0)
    for _ in range(top_k):
        pltpu.make_async_copy(h_ref, xs_ref.at[pl.ds(0, tr * c)], sem).wait()


def _dispatch(pad_start, pad_cnt, n_used, dest_flat, h2_tiles, n_blocks, tr, c):
    T = h2_tiles.shape[0] // c
    n_exp = pad_start.shape[0]
    grid_spec = pltpu.PrefetchScalarGridSpec(
        num_scalar_prefetch=3,
        grid=(T // tr,),
        in_specs=[
            pl.BlockSpec((tr * TOP_K,), lambda i, *_: (i,), memory_space=pltpu.SMEM),
            pl.BlockSpec((tr * c, LANES), lambda i, *_: (i, 0)),
        ],
        out_specs=pl.BlockSpec(memory_space=pl.ANY),
        scratch_shapes=[
            pltpu.VMEM((MOE_ROWS * c, LANES), F32),
            pltpu.SemaphoreType.DMA(()),
            pltpu.SemaphoreType.DMA(()),
        ],
    )
    return pl.pallas_call(
        functools.partial(_dispatch_kernel, n_exp=n_exp, n_blocks=n_blocks, top_k=TOP_K, c=c),
        grid_spec=grid_spec,
        out_shape=jax.ShapeDtypeStruct((n_blocks * MOE_ROWS * c, LANES), F32),
        compiler_params=_params("arbitrary"),
        name="dispatch",
    )(pad_start, pad_cnt, n_used, dest_flat, h2_tiles)


def _expert_kernel(owner_ref, used_ref, x_ref, wu_ref, bu_ref, wd_ref, bd_ref, o_ref, wu_scr, wd_scr, *, d_ff, c):
    i = pl.program_id(0)
    rows = x_ref.shape[0] // c
    used = i < used_ref[0]
    new_expert = (i == 0) | (owner_ref[i] != owner_ref[jnp.maximum(i - 1, 0)])

    @pl.when(used & new_expert)
    def _():
        wu_scr[...] = wu_ref[0].astype(BF16)
        wd_scr[...] = wd_ref[0].astype(BF16)

    @pl.when(used)
    def _():
        pr = rows // EXPERT_PARTS
        for part in range(EXPERT_PARTS):
            rs = pl.ds(part * pr * c, pr * c)
            x = _load_token_tiles(x_ref.at[rs], pr, c).astype(BF16)
            gu = jnp.dot(x, wu_scr[...], preferred_element_type=F32) + bu_ref[0]
            g = jnp.minimum(gu[:, :d_ff], SWIGLU_LIMIT)
            u = jnp.clip(gu[:, d_ff:], -SWIGLU_LIMIT, SWIGLU_LIMIT)
            act = (u + 1.0) * (g * _sigmoid(SWIGLU_ALPHA * g))
            y = jnp.dot(act.astype(BF16), wd_scr[...], preferred_element_type=F32) + bd_ref[0]
            _store_token_tiles(o_ref.at[rs], y)

    @pl.when(i >= used_ref[0])
    def _():
        o_ref[...] = jnp.zeros_like(o_ref)


def _experts(owner, n_used, xs_tiles, w_up, b_up, w_down, b_down):
    n_exp, D, two_f = w_up.shape
    c = D // LANES
    d_ff = two_f // 2
    n_blocks = xs_tiles.shape[0] // (MOE_ROWS * c)
    grid_spec = pltpu.PrefetchScalarGridSpec(
        num_scalar_prefetch=2,
        grid=(n_blocks,),
        in_specs=[
            pl.BlockSpec((MOE_ROWS * c, LANES), lambda i, own, used: (jnp.minimum(i, used[0] - 1), 0)),
            pl.BlockSpec((1, D, two_f), lambda i, own, used: (own[i], 0, 0)),
            pl.BlockSpec((1, 1, two_f), lambda i, own, used: (own[i], 0, 0)),
            pl.BlockSpec((1, d_ff, D), lambda i, own, used: (own[i], 0, 0)),
            pl.BlockSpec((1, 1, D), lambda i, own, used: (own[i], 0, 0)),
        ],
        out_specs=pl.BlockSpec((MOE_ROWS * c, LANES), lambda i, own, used: (i, 0)),
        scratch_shapes=[pltpu.VMEM((D, two_f), BF16), pltpu.VMEM((d_ff, D), BF16)],
    )
    return pl.pallas_call(
        functools.partial(_expert_kernel, d_ff=d_ff, c=c),
        grid_spec=grid_spec,
        out_shape=jax.ShapeDtypeStruct(xs_tiles.shape, F32),
        compiler_params=_params("arbitrary", vmem=VMEM_LIMIT_EXPERTS),
        name="experts",
    )(owner, n_used, xs_tiles, w_up, b_up.reshape(n_exp, 1, two_f), w_down, b_down.reshape(n_exp, 1, D))


def _combine_kernel(dest_ref, gate_ref, h2_ref, g_ref, b_ref, y_ref, o_ref, ybuf, sem, *, alpha, top_k, c, parts):
    tr = h2_ref.shape[0] // c
    pr = tr // parts

    def tile(ref, r):
        return ref.at[pl.ds(pl.multiple_of(r * c, c), c)]

    for part in range(parts):
        def start(j, carry, part=part):
            t0 = j * SEND_GROUP
            base = (part * pr + t0) * top_k
            rows = [dest_ref[base + g * top_k + k] for g in range(SEND_GROUP) for k in range(top_k)]
            for g in range(SEND_GROUP):
                for k in range(top_k):
                    pltpu.make_async_copy(tile(y_ref, rows[g * top_k + k]), tile(ybuf.at[part, k], t0 + g),
                                          sem.at[part]).start(priority=(g * top_k + k) % 2)
            return carry

        lax.fori_loop(0, pr // SEND_GROUP, start, 0)

    for part in range(parts):
        for k in range(top_k):
            pltpu.make_async_copy(y_ref.at[pl.ds(0, pr * c)], ybuf.at[part, k], sem.at[part]).wait()
        rs = slice(part * pr, (part + 1) * pr)
        gate = gate_ref[rs, :]
        f = alpha * _load_token_tiles(h2_ref.at[pl.ds(part * pr * c, pr * c)], pr, c)
        for k in range(top_k):
            f = f + gate[:, k:k + 1] * _load_token_tiles(ybuf.at[part, k], pr, c)
        o_ref[rs, :] = _ln_rows(f, g_ref[...], b_ref[...])


def _combine(dest_flat, gate, h2_tiles, ln_g, ln_b, y_tiles, alpha, tr, c):
    T = h2_tiles.shape[0] // c
    D = c * LANES
    return pl.pallas_call(
        functools.partial(_combine_kernel, alpha=alpha, top_k=TOP_K, c=c, parts=COMBINE_PARTS),
        grid=(T // tr,),
        in_specs=[
            pl.BlockSpec((tr * TOP_K,), lambda i: (i,), memory_space=pltpu.SMEM),
            pl.BlockSpec((tr, LANES), lambda i: (i, 0)),
            pl.BlockSpec((tr * c, LANES), lambda i: (i, 0)),
            pl.BlockSpec((1, D), lambda i: (0, 0)),
            pl.BlockSpec((1, D), lambda i: (0, 0)),
            pl.BlockSpec(memory_space=pl.ANY),
        ],
        out_specs=pl.BlockSpec((tr, D), lambda i: (i, 0)),
        out_shape=jax.ShapeDtypeStruct((T, D), F32),
        scratch_shapes=[pltpu.VMEM((COMBINE_PARTS, TOP_K, tr // COMBINE_PARTS * c, LANES), F32),
                        pltpu.SemaphoreType.DMA((COMBINE_PARTS,))],
        compiler_params=_params("arbitrary"),
        name="combine",
    )(dest_flat, gate, h2_tiles, ln_g, ln_b, y_tiles)


def _tile(n, want):
    t = min(n, want)
    assert n % t == 0, (n, want)
    return t


class _Tiles(NamedTuple):
    in_proj_rows: int
    in_proj_cols: int
    mixer_rows: int
    merge_rows: int
    route_rows: int
    move_rows: int


def _plan(T, n_cols):
    cols = n_cols // 4 if n_cols % (4 * 2 * LANES) == 0 else n_cols
    return _Tiles(_tile(T, 512), cols, _tile(T, 512), _tile(T, 512), _tile(T, 512), _tile(T, ROUTE_ROWS))


def kernel(x, ln_in_g, ln_in_b, w_in, gmlp_ln_g, gmlp_ln_b, w_spatial, b_spatial, w_branch_a, w_branch_b,
           w_out, ln_mix_g, ln_mix_b, w_router, b_router, w_up, b_up, w_down, b_down, ln_ffn_g, ln_ffn_b):
    B, S, D = x.shape
    depth = w_in.shape[0]
    assert depth == 1
    T = B * S
    n_exp = w_router.shape[-1]
    dh = D // N_HEADS
    alpha = (2 * depth) ** 0.25
    assert S % MOBA_BLOCK == 0 and n_exp <= LANES
    row = lambda a: a.reshape(1, -1)

    x2 = x.reshape(T, D)
    tiles = _plan(T, w_in.shape[-1])
    p = _in_proj(x2, row(ln_in_g), row(ln_in_b), w_in[0].astype(BF16), tiles.in_proj_rows, tiles.in_proj_cols)
    y_a = _mixer_a(p, row(gmlp_ln_g[0]), row(gmlp_ln_b[0]), w_spatial[0], b_spatial[0], D, tiles.mixer_rows)
    y_b = _moba(p, B, S, N_HEADS, dh, 2 * N_HEADS, 3 * N_HEADS, 4 * N_HEADS)

    w_r = jnp.zeros((D, LANES), BF16).at[:, :n_exp].set(w_router[0].astype(BF16))
    b_r = jnp.zeros((1, LANES), F32).at[0, :n_exp].set(b_router[0])
    h2, logits = _merge(x2, row(ln_in_g), row(ln_in_b), y_a, y_b, p, 5, 6,
                        w_branch_a[0].astype(BF16), w_branch_b[0].astype(BF16), w_out[0].astype(BF16),
                        row(ln_mix_g[0]), row(ln_mix_b[0]), w_r, b_r, alpha, tiles.merge_rows)

    dest, gate, meta = _route(logits, n_exp, tiles.route_rows)
    n_blocks = -(-T * TOP_K // MOE_ROWS) + n_exp
    counts, starts, ends = meta[0, :n_exp], meta[1, :n_exp], meta[2, :n_exp]
    owner = meta[SUBLANES:].reshape(-1)[:n_blocks]
    n_used = (ends[n_exp - 1:] // MOE_ROWS).astype(I32)
    dest_flat = dest[:, :TOP_K].reshape(-1)

    c = D // LANES
    xs = _dispatch(starts + counts, ends - starts - counts, n_used, dest_flat, h2, n_blocks, tiles.move_rows, c)
    y = _experts(owner, n_used, xs, w_up[0], b_up[0], w_down[0], b_down[0])
    out = _combine(dest_flat, gate, h2, row(ln_ffn_g[0]), row(ln_ffn_b[0]), y, alpha, tiles.move_rows, c)
    return out.reshape(B, S, D)
```

```python
import functools
import math
from typing import NamedTuple

import jax
import jax.numpy as jnp
from jax import lax
from jax.experimental import pallas as pl
from jax.experimental.pallas import tpu as pltpu

F32 = jnp.float32
BF16 = jnp.bfloat16
I32 = jnp.int32

LN_EPS = 1e-5
LOG2_E = math.log2(math.e)
N_HEADS = 8
MOBA_BLOCK = 256
MOBA_TOPK = 3
MOBA_STREAMS = 4
TOP_K = 4
SWIGLU_LIMIT = 7.0
SWIGLU_ALPHA = 1.702
LANES = 128
SUBLANES = 8
MOE_ROWS = 512
DISPATCH_ROWS = 1024
COMBINE_ROWS = 512
SEND_GROUP = 4
COMBINE_PARTS = 4
MERGE_PARTS = 2
EXPERT_PARTS = 2
VMEM_LIMIT = 48 * 1024 * 1024
VMEM_LIMIT_EXPERTS = 56 * 1024 * 1024


def _ln_rows(x, g, b):
    mu = jnp.mean(x, axis=-1, keepdims=True)
    xc = x - mu
    var = jnp.mean(xc * xc, axis=-1, keepdims=True)
    return xc * lax.rsqrt(var + LN_EPS) * g + b


def _gelu(x):
    return x * (lax.erf(x / math.sqrt(2.0)) + 1.0) / 2.0


def _sigmoid(x):
    return 1.0 / (1.0 + jnp.exp(-x))


def _params(*sem, vmem=VMEM_LIMIT):
    return pltpu.CompilerParams(dimension_semantics=sem, vmem_limit_bytes=vmem)


def _store_token_tiles(ref, x):
    rows, d = x.shape
    c = d // LANES
    for s in range(c):
        ref[pl.ds(s, rows, stride=c), :] = x[:, s * LANES:(s + 1) * LANES]


def _load_token_tiles(ref, rows, c):
    return jnp.concatenate([ref[pl.ds(s, rows, stride=c), :] for s in range(c)], axis=1)


def _inproj_kernel(x_ref, g_ref, b_ref, w_ref, o_ref, *, tn):
    h = _ln_rows(x_ref[...], g_ref[...], b_ref[...]).astype(BF16)
    for j in range(w_ref.shape[1] // tn):
        cs = slice(j * tn, (j + 1) * tn)
        o_ref[:, cs] = jnp.dot(h, w_ref[:, cs], preferred_element_type=F32).astype(BF16)


def _in_proj(x2, ln_g, ln_b, w_in, tm, tn):
    T, D = x2.shape
    N = w_in.shape[1]
    assert N % tn == 0
    return pl.pallas_call(
        functools.partial(_inproj_kernel, tn=tn),
        grid=(T // tm,),
        in_specs=[
            pl.BlockSpec((tm, D), lambda i: (i, 0)),
            pl.BlockSpec((1, D), lambda i: (0, 0)),
            pl.BlockSpec((1, D), lambda i: (0, 0)),
            pl.BlockSpec((D, N), lambda i: (0, 0)),
        ],
        out_specs=pl.BlockSpec((tm, N), lambda i: (i, 0)),
        out_shape=jax.ShapeDtypeStruct((T, N), BF16),
        compiler_params=_params("parallel"),
        name="in_proj",
    )(x2, ln_g, ln_b, w_in)


def _mixer_a_kernel(u_ref, v_ref, lng_ref, lnb_ref, ws_ref, bs_ref, o_ref, *, chunk, groups):
    tr, width = u_ref.shape
    gd = width // groups
    u = _gelu(u_ref[...].astype(F32))
    v = _gelu(v_ref[...].astype(F32))
    vn = _ln_rows(v, lng_ref[...], lnb_ref[...]).astype(BF16)
    row = lax.broadcasted_iota(I32, (chunk, chunk), 0)
    col = lax.broadcasted_iota(I32, (chunk, chunk), 1)
    causal = col <= row
    for g in range(groups):
        w = jnp.where(causal, ws_ref[g], 0.0).astype(BF16)
        for c in range(tr // chunk):
            rs = slice(c * chunk, (c + 1) * chunk)
            cs = slice(g * gd, (g + 1) * gd)
            vs = jnp.dot(w, vn[rs, cs], preferred_element_type=F32) + bs_ref[g]
            o_ref[rs, cs] = (u[rs, cs] * vs).astype(BF16)


def _mixer_a(p, ln_g, ln_b, w_s, b_s, width, tr):
    T = p.shape[0]
    groups, chunk, _ = w_s.shape
    gd = width // groups
    bias = jnp.broadcast_to(b_s[:, :, None], (groups, chunk, gd))
    return pl.pallas_call(
        functools.partial(_mixer_a_kernel, chunk=chunk, groups=groups),
        grid=(T // tr,),
        in_specs=[
            pl.BlockSpec((tr, width), lambda i: (i, 0)),
            pl.BlockSpec((tr, width), lambda i: (i, 1)),
            pl.BlockSpec((1, width), lambda i: (0, 0)),
            pl.BlockSpec((1, width), lambda i: (0, 0)),
            pl.BlockSpec((groups, chunk, chunk), lambda i: (0, 0, 0)),
            pl.BlockSpec((groups, chunk, gd), lambda i: (0, 0, 0)),
        ],
        out_specs=pl.BlockSpec((tr, width), lambda i: (i, 0)),
        out_shape=jax.ShapeDtypeStruct((T, width), BF16),
        compiler_params=_params("parallel"),
        name="mixer_a",
    )(p, p, ln_g, ln_b, w_s, bias)


_NT = (((1,), (1,)), ((), ()))


def _moba_kernel(q_ref, k_ref, v_ref, o_ref, kmean_scr, vt_scr, *stream_scr, blk, topk, scale):
    S, dh = q_ref.shape
    nb = S // blk
    for n in range(nb):
        rs = slice(n * blk, (n + 1) * blk)
        kmean_scr[n:n + 1, :] = jnp.mean(k_ref[rs, :].astype(F32), axis=0, keepdims=True)
        vt_scr[:, rs] = v_ref[rs, :].astype(F32).T.astype(BF16)
    kmean = kmean_scr[...].astype(BF16)

    nidx = lax.broadcasted_iota(I32, (nb, blk), 0)
    kidx = lax.broadcasted_iota(I32, (blk, blk), 0)
    qidx = lax.broadcasted_iota(I32, (blk, blk), 1)
    causal = jnp.where(kidx <= qidx, 0.0, -jnp.inf)

    def fold(x, op):
        return op(x.reshape(blk // SUBLANES, SUBLANES, blk), axis=0)

    def query_block(qb, s_scr, p_scr):
        q = q_ref[qb * blk:(qb + 1) * blk, :]
        keys = (qb + 1) * blk
        if qb > 0:
            gate = lax.dot_general(kmean, q, _NT, preferred_element_type=F32)
            past = nidx < qb
            gate = jnp.where(past, gate, -jnp.inf)
            rank = jnp.zeros((nb, blk), I32)
            for m in range(qb):
                gm = gate[m:m + 1, :]
                rank = rank + jnp.where(gm > gate, 1, jnp.where((gm == gate) & (nidx > m), 1, 0))
            bias = jnp.where(past & (rank < topk), 0.0, -jnp.inf)
        yield
        yield
        mx = None
        for n in range(qb + 1):
            rs = slice(n * blk, (n + 1) * blk)
            s = lax.dot_general(k_ref[rs, :], q, _NT, preferred_element_type=F32)
            s = s + (causal if n == qb else bias[n:n + 1, :])
            s_scr[rs, :] = s
            part = fold(s, jnp.max)
            mx = part if mx is None else jnp.maximum(mx, part)
        m_raw = jnp.max(mx, axis=0, keepdims=True)
        yield

        l8 = None
        for n in range(qb + 1):
            rs = slice(n * blk, (n + 1) * blk)
            p = jnp.exp2((s_scr[rs, :] - m_raw) * (scale * LOG2_E))
            p_scr[rs, :] = p.astype(BF16)
            part = fold(p, jnp.sum)
            l8 = part if l8 is None else l8 + part
        l_fin = jnp.sum(l8, axis=0, keepdims=True)
        yield
        acc = jnp.dot(vt_scr[:, 0:keys], p_scr[0:keys, :], preferred_element_type=F32)
        o_ref[qb * blk:(qb + 1) * blk, :] = (acc * (1.0 / l_fin)).T.astype(BF16)
        yield

    n_streams = len(stream_scr) // 2
    order = _moba_order(nb)
    for g in range(0, nb, n_streams):
        streams = [query_block(qb, stream_scr[2 * i], stream_scr[2 * i + 1])
                   for i, qb in enumerate(order[g:g + n_streams])]
        for _ in range(5):
            for st in streams:
                next(st)


def _moba_order(nb):
    order = []
    for a in range((nb + 1) // 2):
        order += [a] if a == nb - 1 - a else [a, nb - 1 - a]
    return order


def _moba(p, B, S, n_heads, dh, q_col, k_col, v_col):
    T = p.shape[0]
    nb = S // MOBA_BLOCK
    kern = functools.partial(_moba_kernel, blk=MOBA_BLOCK, topk=MOBA_TOPK, scale=dh ** -0.5)
    order = _moba_order(nb)
    n_streams = min(MOBA_STREAMS, nb)
    slot_keys = [max(order[i::n_streams]) * MOBA_BLOCK + MOBA_BLOCK for i in range(n_streams)]
    stream_scratch = []
    for keys in slot_keys:
        stream_scratch += [pltpu.VMEM((keys, MOBA_BLOCK), F32),
                           pltpu.VMEM((keys, MOBA_BLOCK), BF16)]
    return pl.pallas_call(
        kern,
        grid=(B, n_heads),
        in_specs=[
            pl.BlockSpec((S, dh), lambda b, h: (b, q_col + h)),
            pl.BlockSpec((S, dh), lambda b, h: (b, k_col + h)),
            pl.BlockSpec((S, dh), lambda b, h: (b, v_col + h)),
        ],
        out_specs=pl.BlockSpec((S, dh), lambda b, h: (b, h)),
        out_shape=jax.ShapeDtypeStruct((T, n_heads * dh), BF16),
        scratch_shapes=[
            pltpu.VMEM((nb, dh), F32),
            pltpu.VMEM((dh, S), BF16),
        ] + stream_scratch,
        compiler_params=_params("parallel", "parallel"),
        name="moba",
    )(p, p, p)


def _merge_kernel(x_ref, lg_ref, lb_ref, ya_ref, yb_ref, ga_ref, gb_ref, wa_ref, wb_ref, wo_ref,
                  mg_ref, mb_ref, wr_ref, br_ref, h2_ref, logit_ref, *, alpha):
    tm, d = x_ref.shape
    c = d // LANES
    pr = tm // MERGE_PARTS
    for part in range(MERGE_PARTS):
        rs = slice(part * pr, (part + 1) * pr)
        h = _ln_rows(x_ref[rs, :], lg_ref[...], lb_ref[...])
        a = jnp.dot(ya_ref[rs, :], wa_ref[...], preferred_element_type=F32)
        b = jnp.dot(yb_ref[rs, :], wb_ref[...], preferred_element_type=F32)
        merged = _sigmoid(ga_ref[rs, :].astype(F32)) * a + _sigmoid(gb_ref[rs, :].astype(F32)) * b
        z = jnp.dot(merged.astype(BF16), wo_ref[...], preferred_element_type=F32)
        h2 = _ln_rows(alpha * h + z, mg_ref[...], mb_ref[...])
        _store_token_tiles(h2_ref.at[pl.ds(part * pr * c, pr * c)], h2)
        logit_ref[rs, :] = jnp.dot(h2.astype(BF16), wr_ref[...], preferred_element_type=F32) + br_ref[...]


def _merge(x2, ln_g, ln_b, y_a, y_b, p, ga_col, gb_col, w_a, w_b, w_o, mix_g, mix_b, w_r, b_r, alpha, tm):
    T, D = x2.shape
    row = lambda i: (i, 0)
    fixed = lambda i: (0, 0)
    return pl.pallas_call(
        functools.partial(_merge_kernel, alpha=alpha),
        grid=(T // tm,),
        in_specs=[
            pl.BlockSpec((tm, D), row),
            pl.BlockSpec((1, D), fixed),
            pl.BlockSpec((1, D), fixed),
            pl.BlockSpec((tm, D), row),
            pl.BlockSpec((tm, D), row),
            pl.BlockSpec((tm, D), lambda i: (i, ga_col)),
            pl.BlockSpec((tm, D), lambda i: (i, gb_col)),
            pl.BlockSpec((D, D), fixed),
            pl.BlockSpec((D, D), fixed),
            pl.BlockSpec((D, D), fixed),
            pl.BlockSpec((1, D), fixed),
            pl.BlockSpec((1, D), fixed),
            pl.BlockSpec((D, LANES), fixed),
            pl.BlockSpec((1, LANES), fixed),
        ],
        out_specs=[pl.BlockSpec((tm * (D // LANES), LANES), row), pl.BlockSpec((tm, LANES), row)],
        out_shape=[jax.ShapeDtypeStruct((T * (D // LANES), LANES), F32),
                   jax.ShapeDtypeStruct((T, LANES), F32)],
        compiler_params=_params("parallel"),
        name="merge",
    )(x2, ln_g, ln_b, y_a, y_b, p, p, w_a, w_b, w_o, mix_g, mix_b, w_r, b_r)


META_ROWS = 2 * SUBLANES


def _route_kernel(logit_ref, dest_ref, gate_ref, meta_ref, cnt_scr, run_scr, start_scr,
                  *, n_exp, top_k, blk_rows):
    phase = pl.program_id(0)
    i = pl.program_id(1)
    tr = logit_ref.shape[0]
    lane = lax.broadcasted_iota(I32, (tr, LANES), 1)
    lane_f = lane.astype(F32)
    lg = jnp.where(lane < n_exp, logit_ref[...], -jnp.inf)
    hots, vals = [], []
    for _ in range(top_k):
        mk = jnp.max(lg, axis=1, keepdims=True)
        ik = jnp.min(jnp.where(lg == mk, lane_f, float(LANES)), axis=1, keepdims=True)
        hot = lane_f == ik
        hots.append(hot)
        vals.append(mk)
        lg = jnp.where(hot, -jnp.inf, lg)
    multi = jnp.zeros((tr, LANES), F32)
    for hot in hots:
        multi = multi + jnp.where(hot, 1.0, 0.0)
    colsum = jnp.sum(multi, axis=0, keepdims=True)

    @pl.when((phase == 0) & (i == 0))
    def _():
        cnt_scr[...] = jnp.zeros_like(cnt_scr)

    @pl.when(phase == 0)
    def _():
        cnt_scr[...] += colsum

    @pl.when((phase == 1) & (i == 0))
    def _():
        lane1 = lax.broadcasted_iota(I32, (1, LANES), 1)
        cnt = cnt_scr[...]
        padded = ((cnt.astype(I32) + (blk_rows - 1)) & (-blk_rows)).astype(F32)
        ends = padded
        shift = 1
        while shift < n_exp:
            ends = ends + jnp.where(lane1 >= shift, pltpu.roll(ends, shift, 1), 0.0)
            shift *= 2
        starts = ends - padded
        start_scr[...] = starts
        run_scr[...] = jnp.zeros_like(run_scr)
        sub = lax.broadcasted_iota(I32, (SUBLANES, LANES), 0)
        lane8 = lax.broadcasted_iota(I32, (SUBLANES, LANES), 1)
        first_row = ((sub * LANES + lane8) * blk_rows).astype(F32)
        owner = jnp.zeros((SUBLANES, LANES), I32)
        for e in range(n_exp):
            end_e = jnp.sum(jnp.where(lane1 == e, ends, 0.0), axis=1, keepdims=True)
            owner = owner + jnp.where(end_e <= first_row, 1, 0)
        meta_ref[...] = jnp.zeros_like(meta_ref)
        meta_ref[0:1, :] = cnt.astype(I32)
        meta_ref[1:2, :] = starts.astype(I32)
        meta_ref[2:3, :] = ends.astype(I32)
        meta_ref[SUBLANES:2 * SUBLANES, :] = jnp.minimum(owner, n_exp - 1)

    @pl.when(phase == 1)
    def _():
        r = lax.broadcasted_iota(I32, (tr, tr), 0)
        c = lax.broadcasted_iota(I32, (tr, tr), 1)
        before = jnp.where(c < r, 1.0, 0.0).astype(BF16)
        rank = jnp.dot(before, multi.astype(BF16), preferred_element_type=F32)
        pos = rank + (start_scr[...] + run_scr[...])
        denom = jnp.zeros((tr, 1), F32)
        exps = []
        for k in range(top_k):
            ek = jnp.exp(vals[k] - vals[0])
            exps.append(ek)
            denom = denom + ek
        dest = jnp.zeros((tr, LANES), F32)
        gate = jnp.zeros((tr, LANES), F32)
        for k in range(top_k):
            dk = jnp.sum(jnp.where(hots[k], pos, 0.0), axis=1, keepdims=True)
            dest = jnp.where(lane == k, dk, dest)
            gate = jnp.where(lane == k, exps[k] / denom, gate)
        dest_ref[...] = dest.astype(I32)
        gate_ref[...] = gate
        run_scr[...] += colsum


def _route(logits, n_exp, tr):
    T = logits.shape[0]
    tile = lambda ph, i: (i * ph, 0)
    return pl.pallas_call(
        functools.partial(_route_kernel, n_exp=n_exp, top_k=TOP_K, blk_rows=MOE_ROWS),
        grid=(2, T // tr),
        in_specs=[pl.BlockSpec((tr, LANES), lambda ph, i: (i, 0))],
        out_specs=[
            pl.BlockSpec((tr, LANES), tile),
            pl.BlockSpec((tr, LANES), tile),
            pl.BlockSpec((META_ROWS, LANES), lambda ph, i: (0, 0)),
        ],
        out_shape=[
            jax.ShapeDtypeStruct((T, LANES), I32),
            jax.ShapeDtypeStruct((T, LANES), F32),
            jax.ShapeDtypeStruct((META_ROWS, LANES), I32),
        ],
        scratch_shapes=[pltpu.VMEM((1, LANES), F32)] * 3,
        compiler_params=_params("arbitrary", "arbitrary"),
        name="route",
    )(logits)


def _dispatch_kernel(pad_start_ref, pad_cnt_ref, used_ref, dest_ref, h_ref, xs_ref, zero_scr, sem, zsem,
                     *, n_exp, n_blocks, top_k, c):
    i = pl.program_id(0)
    tr = h_ref.shape[0] // c
    blk_rows = zero_scr.shape[0]

    def tile(ref, r):
        return ref.at[pl.ds(pl.multiple_of(r * c, c), c)]

    def zero_rows(dst_row, n_rows):
        dst = xs_ref.at[pl.ds(pl.multiple_of(dst_row * c, c), n_rows * c)]
        return pltpu.make_async_copy(zero_scr.at[pl.ds(0, n_rows * c)], dst, zsem)

    def zero_pad(e, wait):
        at = pad_start_ref[e]
        left = pad_cnt_ref[e]
        size = blk_rows // c // 2
        while size >= 1:
            has = (left & size) != 0

            @pl.when(has)
            def _(at=at, size=size):
                cp = zero_rows(at, size)
                cp.wait() if wait else cp.start()

            at = at + jnp.where(has, size, 0)
            size //= 2

    def token_row(t, dst_row):
        return pltpu.make_async_copy(tile(h_ref, t), tile(xs_ref, dst_row), sem)

    @pl.when(i == 0)
    def _():
        zero_scr[...] = jnp.zeros_like(zero_scr)

        def start_pad(e, carry):
            zero_pad(e, wait=False)
            return carry

        def wait_pad(e, carry):
            zero_pad(e, wait=True)
            return carry

        lax.fori_loop(0, n_exp, start_pad, 0)
        lax.fori_loop(0, n_exp, wait_pad, 0)

        def per_block(b, c):
            r0 = pl.multiple_of(b * blk_rows, blk_rows)
            cp = pltpu.make_async_copy(zero_scr, xs_ref.at[pl.ds(r0, blk_rows)], zsem)
            cp.start()
            cp.wait()
            return c

        lax.fori_loop(used_ref[0], n_blocks, per_block, 0)

    def send(j, carry):
        t0 = j * SEND_GROUP
        rows = [dest_ref[(t0 + g) * top_k + k] for g in range(SEND_GROUP) for k in range(top_k)]
        for g in range(SEND_GROUP):
            for k in range(top_k):
                token_row(t0 + g, rows[g * top_k + k]).start(priority=(g * top_k + k) % 2)
        return carry

    lax.fori_loop(0, tr // SEND_GROUP, send, 0)
    for _ in range(top_k):
        pltpu.make_async_copy(h_ref, xs_ref.at[pl.ds(0, tr * c)], sem).wait()


def _dispatch(pad_start, pad_cnt, n_used, dest_flat, h2_tiles, n_blocks, tr, c):
    T = h2_tiles.shape[0] // c
    n_exp = pad_start.shape[0]
    grid_spec = pltpu.PrefetchScalarGridSpec(
        num_scalar_prefetch=3,
        grid=(T // tr,),
        in_specs=[
            pl.BlockSpec((tr * TOP_K,), lambda i, *_: (i,), memory_space=pltpu.SMEM),
            pl.BlockSpec((tr * c, LANES), lambda i, *_: (i, 0)),
        ],
        out_specs=pl.BlockSpec(memory_space=pl.ANY),
        scratch_shapes=[
            pltpu.VMEM((MOE_ROWS * c, LANES), F32),
            pltpu.SemaphoreType.DMA(()),
            pltpu.SemaphoreType.DMA(()),
        ],
    )
    return pl.pallas_call(
        functools.partial(_dispatch_kernel, n_exp=n_exp, n_blocks=n_blocks, top_k=TOP_K, c=c),
        grid_spec=grid_spec,
        out_shape=jax.ShapeDtypeStruct((n_blocks * MOE_ROWS * c, LANES), F32),
        compiler_params=_params("arbitrary"),
        name="dispatch",
    )(pad_start, pad_cnt, n_used, dest_flat, h2_tiles)


def _expert_kernel(owner_ref, used_ref, x_ref, wu_ref, bu_ref, wd_ref, bd_ref, o_ref, wu_scr, wd_scr, *, d_ff, c):
    i = pl.program_id(0)
    rows = x_ref.shape[0] // c
    used = i < used_ref[0]
    new_expert = (i == 0) | (owner_ref[i] != owner_ref[jnp.maximum(i - 1, 0)])

    @pl.when(used & new_expert)
    def _():
        wu_scr[...] = wu_ref[0].astype(BF16)
        wd_scr[...] = wd_ref[0].astype(BF16)

    @pl.when(used)
    def _():
        pr = rows // EXPERT_PARTS
        for part in range(EXPERT_PARTS):
            rs = pl.ds(part * pr * c, pr * c)
            x = _load_token_tiles(x_ref.at[rs], pr, c).astype(BF16)
            gu = jnp.dot(x, wu_scr[...], preferred_element_type=F32) + bu_ref[0]
            g = jnp.minimum(gu[:, :d_ff], SWIGLU_LIMIT)
            u = jnp.clip(gu[:, d_ff:], -SWIGLU_LIMIT, SWIGLU_LIMIT)
            act = (u + 1.0) * (g * _sigmoid(SWIGLU_ALPHA * g))
            y = jnp.dot(act.astype(BF16), wd_scr[...], preferred_element_type=F32) + bd_ref[0]
            _store_token_tiles(o_ref.at[rs], y)

    @pl.when(i >= used_ref[0])
    def _():
        o_ref[...] = jnp.zeros_like(o_ref)


def _experts(owner, n_used, xs_tiles, w_up, b_up, w_down, b_down):
    n_exp, D, two_f = w_up.shape
    c = D // LANES
    d_ff = two_f // 2
    n_blocks = xs_tiles.shape[0] // (MOE_ROWS * c)
    grid_spec = pltpu.PrefetchScalarGridSpec(
        num_scalar_prefetch=2,
        grid=(n_blocks,),
        in_specs=[
            pl.BlockSpec((MOE_ROWS * c, LANES), lambda i, own, used: (jnp.minimum(i, used[0] - 1), 0)),
            pl.BlockSpec((1, D, two_f), lambda i, own, used: (own[i], 0, 0)),
            pl.BlockSpec((1, 1, two_f), lambda i, own, used: (own[i], 0, 0)),
            pl.BlockSpec((1, d_ff, D), lambda i, own, used: (own[i], 0, 0)),
            pl.BlockSpec((1, 1, D), lambda i, own, used: (own[i], 0, 0)),
        ],
        out_specs=pl.BlockSpec((MOE_ROWS * c, LANES), lambda i, own, used: (i, 0)),
        scratch_shapes=[pltpu.VMEM((D, two_f), BF16), pltpu.VMEM((d_ff, D), BF16)],
    )
    return pl.pallas_call(
        functools.partial(_expert_kernel, d_ff=d_ff, c=c),
        grid_spec=grid_spec,
        out_shape=jax.ShapeDtypeStruct(xs_tiles.shape, F32),
        compiler_params=_params("arbitrary", vmem=VMEM_LIMIT_EXPERTS),
        name="experts",
    )(owner, n_used, xs_tiles, w_up, b_up.reshape(n_exp, 1, two_f), w_down, b_down.reshape(n_exp, 1, D))


def _combine_kernel(dest_ref, gate_ref, h2_ref, g_ref, b_ref, y_ref, o_ref, ybuf, sem, *, alpha, top_k, c, parts):
    tr = h2_ref.shape[0] // c
    pr = tr // parts

    def tile(ref, r):
        return ref.at[pl.ds(pl.multiple_of(r * c, c), c)]

    for part in range(parts):
        def start(j, carry, part=part):
            t0 = j * SEND_GROUP
            base = (part * pr + t0) * top_k
            rows = [dest_ref[base + g * top_k + k] for g in range(SEND_GROUP) for k in range(top_k)]
            for g in range(SEND_GROUP):
                for k in range(top_k):
                    pltpu.make_async_copy(tile(y_ref, rows[g * top_k + k]), tile(ybuf.at[part, k], t0 + g),
                                          sem.at[part]).start(priority=(g * top_k + k) % 2)
            return carry

        lax.fori_loop(0, pr // SEND_GROUP, start, 0)

    for part in range(parts):
        for k in range(top_k):
            pltpu.make_async_copy(y_ref.at[pl.ds(0, pr * c)], ybuf.at[part, k], sem.at[part]).wait()
        rs = slice(part * pr, (part + 1) * pr)
        gate = gate_ref[rs, :]
        f = alpha * _load_token_tiles(h2_ref.at[pl.ds(part * pr * c, pr * c)], pr, c)
        for k in range(top_k):
            f = f + gate[:, k:k + 1] * _load_token_tiles(ybuf.at[part, k], pr, c)
        o_ref[rs, :] = _ln_rows(f, g_ref[...], b_ref[...])


def _combine(dest_flat, gate, h2_tiles, ln_g, ln_b, y_tiles, alpha, tr, c):
    T = h2_tiles.shape[0] // c
    D = c * LANES
    return pl.pallas_call(
        functools.partial(_combine_kernel, alpha=alpha, top_k=TOP_K, c=c, parts=COMBINE_PARTS),
        grid=(T // tr,),
        in_specs=[
            pl.BlockSpec((tr * TOP_K,), lambda i: (i,), memory_space=pltpu.SMEM),
            pl.BlockSpec((tr, LANES), lambda i: (i, 0)),
            pl.BlockSpec((tr * c, LANES), lambda i: (i, 0)),
            pl.BlockSpec((1, D), lambda i: (0, 0)),
            pl.BlockSpec((1, D), lambda i: (0, 0)),
            pl.BlockSpec(memory_space=pl.ANY),
        ],
        out_specs=pl.BlockSpec((tr, D), lambda i: (i, 0)),
        out_shape=jax.ShapeDtypeStruct((T, D), F32),
        scratch_shapes=[pltpu.VMEM((COMBINE_PARTS, TOP_K, tr // COMBINE_PARTS * c, LANES), F32),
                        pltpu.SemaphoreType.DMA((COMBINE_PARTS,))],
        compiler_params=_params("arbitrary"),
        name="combine",
    )(dest_flat, gate, h2_tiles, ln_g, ln_b, y_tiles)


def _tile(n, want):
    t = min(n, want)
    assert n % t == 0, (n, want)
    return t


class _Tiles(NamedTuple):
    in_proj_rows: int
    in_proj_cols: int
    mixer_rows: int
    merge_rows: int
    route_rows: int
    dispatch_rows: int
    combine_rows: int


def _plan(T, n_cols):
    cols = n_cols // 4 if n_cols % (4 * 2 * LANES) == 0 else n_cols
    return _Tiles(_tile(T, 512), cols, _tile(T, 512), _tile(T, 512), _tile(T, 512),
                  _tile(T, DISPATCH_ROWS), _tile(T, COMBINE_ROWS))


def kernel(x, ln_in_g, ln_in_b, w_in, gmlp_ln_g, gmlp_ln_b, w_spatial, b_spatial, w_branch_a, w_branch_b,
           w_out, ln_mix_g, ln_mix_b, w_router, b_router, w_up, b_up, w_down, b_down, ln_ffn_g, ln_ffn_b):
    B, S, D = x.shape
    depth = w_in.shape[0]
    assert depth == 1
    T = B * S
    n_exp = w_router.shape[-1]
    dh = D // N_HEADS
    alpha = (2 * depth) ** 0.25
    assert S % MOBA_BLOCK == 0 and n_exp <= LANES
    row = lambda a: a.reshape(1, -1)

    x2 = x.reshape(T, D)
    tiles = _plan(T, w_in.shape[-1])
    p = _in_proj(x2, row(ln_in_g), row(ln_in_b), w_in[0].astype(BF16), tiles.in_proj_rows, tiles.in_proj_cols)
    y_a = _mixer_a(p, row(gmlp_ln_g[0]), row(gmlp_ln_b[0]), w_spatial[0], b_spatial[0], D, tiles.mixer_rows)
    y_b = _moba(p, B, S, N_HEADS, dh, 2 * N_HEADS, 3 * N_HEADS, 4 * N_HEADS)

    w_r = jnp.zeros((D, LANES), BF16).at[:, :n_exp].set(w_router[0].astype(BF16))
    b_r = jnp.zeros((1, LANES), F32).at[0, :n_exp].set(b_router[0])
    h2, logits = _merge(x2, row(ln_in_g), row(ln_in_b), y_a, y_b, p, 5, 6,
                        w_branch_a[0].astype(BF16), w_branch_b[0].astype(BF16), w_out[0].astype(BF16),
                        row(ln_mix_g[0]), row(ln_mix_b[0]), w_r, b_r, alpha, tiles.merge_rows)

    dest, gate, meta = _route(logits, n_exp, tiles.route_rows)
    n_blocks = -(-T * TOP_K // MOE_ROWS) + n_exp
    counts, starts, ends = meta[0, :n_exp], meta[1, :n_exp], meta[2, :n_exp]
    owner = meta[SUBLANES:].reshape(-1)[:n_blocks]
    n_used = (ends[n_exp - 1:] // MOE_ROWS).astype(I32)
    dest_flat = dest[:, :TOP_K].reshape(-1)

    c = D // LANES
    xs = _dispatch(starts + counts, ends - starts - counts, n_used, dest_flat, h2, n_blocks, tiles.dispatch_rows, c)
    y = _experts(owner, n_used, xs, w_up[0], b_up[0], w_down[0], b_down[0])
    out = _combine(dest_flat, gate, h2, row(ln_ffn_g[0]), row(ln_ffn_b[0]), y, alpha, tiles.combine_rows, c)
    return out.reshape(B, S, D)
```
